```python
import jax
import jax.numpy as jnp
from jax import lax
import numpy as np

D_MODEL = 1024
BATCH = 1
SEQ = 16384
DEPTH = 4

GRID_W = 64
CTX_LEN = 256
HEAD_DIM = 64
ATTN_W = D_MODEL // 2
N_Q_HEADS = ATTN_W // HEAD_DIM
N_KV_HEADS = N_Q_HEADS // 4
Q_PER_KV = N_Q_HEADS // N_KV_HEADS
KV_W = N_KV_HEADS * HEAD_DIM
AXIS_DIM = HEAD_DIM // 2
ROPE_BASE = 10000.0
Q_BLOCK = 128
SGU_W = D_MODEL // 4
SGU_GROUP_W = 64
SGU_GROUPS = SGU_W // SGU_GROUP_W
CHUNK = 128
FNET_W = D_MODEL // 4
FNET_GROUP_W = 64
FNET_GROUPS = FNET_W // FNET_GROUP_W
MIX_W = ATTN_W + SGU_W + FNET_W
IN_W = ATTN_W + 2 * KV_W + 2 * SGU_W + FNET_W
N_EXPERTS = 16
CAPACITY_FACTOR = 2
EXPERT_FF = 2 * D_MODEL
EPS = 1e-6

kernel_name = 'hybrid_parallel_groups_ec_moe_dit'


def rms_norm(x, g):
    x32 = x.astype(jnp.float32)
    y = x32 * lax.rsqrt(jnp.mean(x32 * x32, axis=-1, keepdims=True) + EPS)
    return (y * g.astype(jnp.float32)).astype(x.dtype)


def modulate(h, shift, scale):
    return h * (1 + scale) + shift


def axial_angles(n):
    rows = n // GRID_W
    r, col = jnp.meshgrid(jnp.arange(rows, dtype=jnp.float32),
                          jnp.arange(GRID_W, dtype=jnp.float32), indexing='ij')
    inv = 1.0 / (ROPE_BASE ** (jnp.arange(0, AXIS_DIM, 2, dtype=jnp.float32) / AXIS_DIM))
    return r.reshape(-1, 1) * inv, col.reshape(-1, 1) * inv


def rope_1d(x, ang):
    x1, x2 = jnp.split(x, 2, axis=-1)
    cos, sin = jnp.cos(ang), jnp.sin(ang)
    return jnp.concatenate([x1 * cos - x2 * sin, x1 * sin + x2 * cos], axis=-1)


def rope_2d(x, ang_row, ang_col):
    x32 = x.astype(jnp.float32)
    xr, xc = jnp.split(x32, 2, axis=-1)
    out = jnp.concatenate([rope_1d(xr, ang_row[:, None, :]), rope_1d(xc, ang_col[:, None, :])], axis=-1)
    return out.astype(x.dtype)


def split_projection(p):
    B, N, _ = p.shape
    o1 = ATTN_W
    o2 = o1 + KV_W
    o3 = o2 + KV_W
    o4 = o3 + 2 * SGU_W
    q, k, v, z_sgu, z_fft = jnp.split(p, [o1, o2, o3, o4], axis=-1)
    return (q.reshape(B, N, N_Q_HEADS, HEAD_DIM),
            k.reshape(B, N, N_KV_HEADS, HEAD_DIM),
            v.reshape(B, N, N_KV_HEADS, HEAD_DIM),
            z_sgu, z_fft)


def q_layout(q):
    B, N, _, _ = q.shape
    return q.reshape(B, N, N_KV_HEADS, Q_PER_KV, HEAD_DIM).transpose(0, 2, 3, 1, 4)


def kv_layout(k):
    return k.transpose(0, 2, 1, 3)


def attend(q, k, v):
    s = jnp.einsum('bkgqd,bksd->bkgqs', q, k).astype(jnp.float32) * (HEAD_DIM ** -0.5)
    p = jax.nn.softmax(s, axis=-1).astype(v.dtype)
    return jnp.einsum('bkgqs,bksd->bkgqd', p, v)


def attend_blocked(q, k, v):
    B, Hk, G, N, Dh = q.shape
    nb = N // Q_BLOCK
    qb = q.reshape(B, Hk, G, nb, Q_BLOCK, Dh).transpose(3, 0, 1, 2, 4, 5)
    ob = lax.map(lambda blk: attend(blk, k, v), qb)
    return ob.transpose(1, 2, 3, 0, 4, 5).reshape(B, Hk, G, N, Dh)


def heads_to_tokens(o):
    B, Hk, G, N, Dh = o.shape
    return o.transpose(0, 3, 1, 2, 4).reshape(B, N, Hk * G * Dh)


def spatial_gating(z, w_s, b_s, g):
    B, N, _ = z.shape
    u, v = jnp.split(jax.nn.gelu(z), 2, axis=-1)
    v = rms_norm(v.reshape(B, N, SGU_GROUPS, SGU_GROUP_W), g.reshape(SGU_GROUPS, SGU_GROUP_W))
    v = v.reshape(B, N // CHUNK, CHUNK, SGU_GROUPS, SGU_GROUP_W)
    s = jnp.einsum('gpq,bnqgc->bnpgc', w_s, v) + b_s.T[:, :, None]
    return u * s.reshape(B, N, SGU_W)


def fourier_mix(z):
    B, N, _ = z.shape
    f = z.reshape(B, N, FNET_GROUPS, FNET_GROUP_W).astype(jnp.float32)
    y = jnp.fft.fft2(f, axes=(1, 3), norm='ortho').real
    return y.reshape(B, N, FNET_W).astype(z.dtype)


def token_mix(q, k, v, z_sgu, z_fft, blocked, sgu_w, sgu_b, sgu_g, w_out):
    o = attend_blocked(q, k, v) if blocked else attend(q, k, v)
    mix = jnp.concatenate([heads_to_tokens(o),
                           spatial_gating(z_sgu, sgu_w, sgu_b, sgu_g),
                           fourier_mix(z_fft)], axis=-1)
    return mix @ w_out


def expert_choice_ffn(h, router_w, w_gate, w_up, w_down):
    B, N, D = h.shape
    cap = CAPACITY_FACTOR * N // N_EXPERTS
    aff = jax.nn.softmax(jnp.einsum('bnd,de->bne', h, router_w).astype(jnp.float32), axis=-1)
    gates, idx = lax.top_k(aff.transpose(0, 2, 1), cap)
    xs = jax.vmap(lambda hb, ib: hb[ib])(h, idx)
    a = jnp.einsum('becd,edf->becf', xs, w_gate)
    u = jnp.einsum('becd,edf->becf', xs, w_up)
    y = jnp.einsum('becf,efd->becd', jax.nn.silu(a) * u, w_down) * gates[..., None].astype(h.dtype)
    return jax.vmap(lambda ib, yb: jnp.zeros((N, D), yb.dtype).at[ib.reshape(-1)].add(yb.reshape(-1, D)))(idx, y)


def setup_inputs(seed: int = 0) -> dict:
    key = jax.random.key(seed)
    ks = jax.random.split(key, 20)
    f32 = jnp.float32

    def nrm(k, shape, scale=1.0):
        return jax.random.normal(k, shape, f32) * scale

    def gain(k, shape):
        return 1.0 + nrm(k, shape, 0.02)

    D = D_MODEL
    return {
        'x': nrm(ks[0], (BATCH, SEQ, D)),
        'c': nrm(ks[1], (BATCH, D)),
        'ctx': nrm(ks[2], (BATCH, CTX_LEN, D)),
        'c_ctx': nrm(ks[3], (D,)),
        'w_ada': nrm(ks[4], (DEPTH, D, 6 * D), D ** -0.5),
        'b_ada': nrm(ks[5], (DEPTH, 6 * D), 0.02),
        'norm1_g': gain(ks[6], (DEPTH, D)),
        'w_in': nrm(ks[7], (DEPTH, D, IN_W), D ** -0.5),
        'q_norm_g': gain(ks[8], (DEPTH, HEAD_DIM)),
        'k_norm_g': gain(ks[9], (DEPTH, HEAD_DIM)),
        'sgu_norm_g': gain(ks[10], (DEPTH, SGU_W)),
        'sgu_w': nrm(ks[11], (DEPTH, SGU_GROUPS, CHUNK, CHUNK), CHUNK ** -0.5),
        'sgu_b': gain(ks[12], (DEPTH, SGU_GROUPS, CHUNK)),
        'w_out': nrm(ks[13], (DEPTH, MIX_W, D), MIX_W ** -0.5),
        'norm2_g': gain(ks[14], (DEPTH, D)),
        'router_w': nrm(ks[15], (DEPTH, D, N_EXPERTS), D ** -0.5),
        'w_gate': nrm(ks[16], (DEPTH, N_EXPERTS, D, EXPERT_FF), D ** -0.5),
        'w_up': nrm(ks[17], (DEPTH, N_EXPERTS, D, EXPERT_FF), D ** -0.5),
        'w_down': nrm(ks[18], (DEPTH, N_EXPERTS, EXPERT_FF, D), EXPERT_FF ** -0.5),
        'final_g': gain(ks[19], (D,)),
    }


def reference(x, c, ctx, c_ctx, w_ada, b_ada, norm1_g, w_in, q_norm_g, k_norm_g, sgu_norm_g,
              sgu_w, sgu_b, w_out, norm2_g, router_w, w_gate, w_up, w_down, final_g):
    L = x.shape[1]
    ang_r, ang_c = axial_angles(L)
    s_lat = jax.nn.silu(c)
    s_ctx = jax.nn.silu(c_ctx)
    xc = ctx
    for i in range(DEPTH):
        last = i == DEPTH - 1
        m_l = jnp.split((s_lat @ w_ada[i] + b_ada[i])[:, None, :], 6, axis=-1)
        m_c = jnp.split(s_ctx @ w_ada[i] + b_ada[i], 6, axis=-1)

        h_l = modulate(rms_norm(x, norm1_g[i]), m_l[0], m_l[1])
        h_c = modulate(rms_norm(xc, norm1_g[i]), m_c[0], m_c[1])
        q_l, k_l, v_l, zs_l, zf_l = split_projection(h_l @ w_in[i])
        q_c, k_c, v_c, zs_c, zf_c = split_projection(h_c @ w_in[i])
        q_l = q_layout(rope_2d(rms_norm(q_l, q_norm_g[i]), ang_r, ang_c))
        k_l = kv_layout(rope_2d(rms_norm(k_l, k_norm_g[i]), ang_r, ang_c))
        k_c = kv_layout(rms_norm(k_c, k_norm_g[i]))
        v_c = kv_layout(v_c)
        k_all = jnp.concatenate([k_c, k_l], axis=2)
        v_all = jnp.concatenate([v_c, kv_layout(v_l)], axis=2)
        x = x + m_l[2] * token_mix(q_l, k_all, v_all, zs_l, zf_l, True,
                                   sgu_w[i], sgu_b[i], sgu_norm_g[i], w_out[i])

        h2 = modulate(rms_norm(x, norm2_g[i]), m_l[3], m_l[4])
        x = x + m_l[5] * expert_choice_ffn(h2, router_w[i], w_gate[i], w_up[i], w_down[i])

        if not last:
            q_c = q_layout(rms_norm(q_c, q_norm_g[i]))
            xc = xc + m_c[2] * token_mix(q_c, k_c, v_c, zs_c, zf_c, False,
                                         sgu_w[i], sgu_b[i], sgu_norm_g[i], w_out[i])
            h2c = modulate(rms_norm(xc, norm2_g[i]), m_c[3], m_c[4])
            xc = xc + m_c[5] * expert_choice_ffn(h2c, router_w[i], w_gate[i], w_up[i], w_down[i])
    return rms_norm(x, final_g)
```

```python
import functools
import math

import numpy as np
import jax
import jax.numpy as jnp
from jax import lax
from jax.experimental import pallas as pl
from jax.experimental.pallas import tpu as pltpu

D_MODEL = 1024
DEPTH = 4
GRID_W = 64
HEAD_DIM = 64
ATTN_W = 512
N_Q_HEADS = 8
N_KV_HEADS = 2
Q_PER_KV = 4
KV_W = 128
AXIS_DIM = 32
ROPE_BASE = 10000.0
SGU_W = 256
SGU_GROUP_W = 64
SGU_GROUPS = 4
CHUNK = 128
FNET_W = 256
FNET_GROUP_W = 64
IN_W = 1536
N_EXPERTS = 16
CAPACITY_FACTOR = 2
EXPERT_FF = 2048
EPS = 1e-6

LANES = 128
TM = 256
TK = 512
FF_TILE = 512
COMBINE_WIN = 384
WIN_ALIGN = 16
CTX_ROUTE_ROWS = 16
VMEM_LIMIT = 56 * 1024 * 1024

F32 = jnp.float32
BF16 = jnp.bfloat16


def _dot(a, b):
    return jnp.dot(a, b, preferred_element_type=F32)


def _dot_nt(a, b):
    return lax.dot_general(a, b, (((1,), (1,)), ((), ())), preferred_element_type=F32)


def _split(a):
    hi = a.astype(BF16)
    lo = (a - hi.astype(F32)).astype(BF16)
    return hi, lo


def _dot_x2(a, b_bf16):
    hi, lo = _split(a)
    return _dot(hi, b_bf16) + _dot(lo, b_bf16)


def _dot_x3(a, b):
    ah, al = _split(a)
    bh, bl = _split(b)
    return _dot(ah, bh) + (_dot(al, bh) + _dot(ah, bl))


def _dot_nt_x3(a, b):
    ah, al = _split(a)
    bh, bl = _split(b)
    return _dot_nt(ah, bh) + (_dot_nt(al, bh) + _dot_nt(ah, bl))


def _sigmoid(a):
    return 1.0 / (1.0 + jnp.exp(-a))


def _params(sem, vmem=VMEM_LIMIT):
    return pltpu.CompilerParams(dimension_semantics=sem, vmem_limit_bytes=vmem)


def _blockdiag(block, n):
    k = block.shape[0]
    out = np.zeros((k * n, block.shape[1] * n), np.float64)
    for i in range(n):
        out[i * k:(i + 1) * k, i * block.shape[1]:(i + 1) * block.shape[1]] = block
    return out


def _dft_cs(n):
    a = 2.0 * np.pi * np.outer(np.arange(n), np.arange(n)) / n
    return np.cos(a), np.sin(a)


def _rot_matrix(width):
    h = AXIS_DIM // 2
    r = np.zeros((width, width), np.float64)
    for j in range(width):
        if (j % AXIS_DIM) < h:
            r[j + h, j] = -1.0
        else:
            r[j - h, j] = 1.0
    return r


def _rope_tables(L, C):
    rows = L // GRID_W
    r, col = jnp.meshgrid(jnp.arange(rows, dtype=F32), jnp.arange(GRID_W, dtype=F32), indexing='ij')
    inv = 1.0 / (ROPE_BASE ** (jnp.arange(0, AXIS_DIM, 2, dtype=F32) / AXIS_DIM))
    ar = r.reshape(-1, 1) * inv
    ac = col.reshape(-1, 1) * inv
    cos = jnp.concatenate([jnp.cos(ar), jnp.cos(ar), jnp.cos(ac), jnp.cos(ac)], axis=-1)
    sin = jnp.concatenate([jnp.sin(ar), jnp.sin(ar), jnp.sin(ac), jnp.sin(ac)], axis=-1)
    cos = jnp.concatenate([cos, jnp.ones((C, HEAD_DIM), F32)], axis=0)
    sin = jnp.concatenate([sin, jnp.zeros((C, HEAD_DIM), F32)], axis=0)
    return jnp.tile(cos, (1, 2)), jnp.tile(sin, (1, 2))


def _mods_kernel(cv_ref, w_ref, b_ref, o_ref):
    cv = cv_ref[...]
    s = cv * _sigmoid(cv)
    o_ref[...] = _dot_x3(s, w_ref[...]) + b_ref[...]


def _mods(cvec, w_ada, b_ada):
    depth, d, six_d = w_ada.shape
    rows = cvec.shape[0]
    tn = 1024
    return pl.pallas_call(
        _mods_kernel,
        grid=(depth, six_d // tn),
        in_specs=[pl.BlockSpec((rows, d), lambda l, j: (0, 0)),
                  pl.BlockSpec((None, d, tn), lambda l, j: (l, 0, j)),
                  pl.BlockSpec((None, 1, tn), lambda l, j: (l, 0, j))],
        out_specs=pl.BlockSpec((None, rows, tn), lambda l, j: (l, 0, j)),
        out_shape=jax.ShapeDtypeStruct((depth, rows, six_d), F32),
        compiler_params=_params(("arbitrary", "arbitrary")),
    )(cvec, w_ada, b_ada.reshape(depth, 1, six_d))


def _head_norm_rope(t, gain, cos, sin, msq, rot):
    ms = _dot_x2(t * t, msq)
    tn = t * lax.rsqrt(ms + EPS) * gain
    return tn * cos + _dot(tn.astype(BF16), rot) * sin


def _gelu_tanh(z):
    return 0.5 * z * (1.0 + jnp.tanh(math.sqrt(2.0 / math.pi) * (z + 0.044715 * (z * z * z))))


def _proj_kernel(x_ref, mod_ref, g1_ref, win_ref, cos_ref, sin_ref, qg_ref, kg_ref, msq_ref, rot_ref,
                 sgug_ref, wcat_ref, sbias_ref, dft_ref,
                 q_ref, kt_ref, v_ref, sgu_ref, g_ref):
    x = x_ref[...]
    shift = mod_ref[0:1, :]
    scale = mod_ref[1:2, :]
    h = x * lax.rsqrt(jnp.mean(x * x, axis=-1, keepdims=True) + EPS) * g1_ref[...]
    h = h * (1.0 + scale) + shift
    p = _dot(h.astype(BF16), win_ref[...])

    cos2 = cos_ref[...]
    sin2 = sin_ref[...]
    cos8 = jnp.concatenate([cos2] * 4, axis=1)
    sin8 = jnp.concatenate([sin2] * 4, axis=1)
    qo = _head_norm_rope(p[:, :ATTN_W], qg_ref[...], cos8, sin8, msq_ref[...], rot_ref[...])
    qo = qo * (HEAD_DIM ** -0.5)
    for hh in range(N_Q_HEADS):
        q_ref[hh] = qo[:, hh * HEAD_DIM:(hh + 1) * HEAD_DIM].astype(BF16)

    ko = _head_norm_rope(p[:, ATTN_W:ATTN_W + KV_W], kg_ref[...], cos2, sin2,
                         msq_ref[:KV_W, :KV_W], rot_ref[:KV_W, :KV_W])
    kot = ko.T
    for hh in range(N_KV_HEADS):
        for cc in range(TM // LANES):
            kt_ref[hh, cc] = kot[hh * HEAD_DIM:(hh + 1) * HEAD_DIM, cc * LANES:(cc + 1) * LANES].astype(BF16)
        o = ATTN_W + KV_W + hh * HEAD_DIM
        v_ref[hh] = p[:, o:o + HEAD_DIM].astype(BF16)

    o3 = ATTN_W + 2 * KV_W
    ge = _gelu_tanh(p[:, o3:o3 + 2 * SGU_W])
    u = ge[:, :SGU_W]
    vv = ge[:, SGU_W:]
    vms = _dot_x2(vv * vv, msq_ref[:SGU_W, :SGU_W])
    vn = vv * lax.rsqrt(vms + EPS) * sgug_ref[...]
    lane_group = lax.broadcasted_iota(jnp.int32, (CHUNK, SGU_W), 1) // SGU_GROUP_W
    for cc in range(TM // CHUNK):
        vc = vn[cc * CHUNK:(cc + 1) * CHUNK]
        vstack = jnp.concatenate([jnp.where(lane_group == gg, vc, 0.0) for gg in range(SGU_GROUPS)],
                                 axis=0).astype(BF16)
        s = _dot(wcat_ref[...], vstack) + sbias_ref[...]
        sgu_ref[cc * CHUNK:(cc + 1) * CHUNK, :] = (u[cc * CHUNK:(cc + 1) * CHUNK] * s).astype(BF16)

    o4 = o3 + 2 * SGU_W
    g_ref[...] = _dot(p[:, o4:].astype(BF16), dft_ref[...]).astype(BF16)


def _proj(layer, t, mods, norm1_g, w_in_bf, cos_t, sin_t, qg, kg, msq, rot, sgug, wcat, sbias, dftc, nl):
    n = t.shape[0]
    nt = n // TM
    const = lambda *shape: pl.BlockSpec(shape, lambda i: (0,) * len(shape))
    lay = lambda *shape: pl.BlockSpec((None,) + shape, lambda i: (layer,) + (0,) * len(shape))
    return pl.pallas_call(
        _proj_kernel,
        grid=(nt,),
        in_specs=[pl.BlockSpec((TM, D_MODEL), lambda i: (i, 0)),
                  pl.BlockSpec((None, None, 6, D_MODEL), lambda i: (layer, i // nl, 0, 0)),
                  lay(1, D_MODEL),
                  lay(D_MODEL, IN_W),
                  pl.BlockSpec((TM, 2 * HEAD_DIM), lambda i: (i, 0)),
                  pl.BlockSpec((TM, 2 * HEAD_DIM), lambda i: (i, 0)),
                  lay(1, ATTN_W), lay(1, KV_W),
                  const(ATTN_W, ATTN_W), const(ATTN_W, ATTN_W),
                  lay(1, SGU_W), lay(CHUNK, SGU_GROUPS * CHUNK), lay(CHUNK, SGU_W),
                  const(FNET_W, 2 * FNET_W)],
        out_specs=[pl.BlockSpec((N_Q_HEADS, TM, HEAD_DIM), lambda i: (0, i, 0)),
                   pl.BlockSpec((N_KV_HEADS, TM // LANES, HEAD_DIM, LANES), lambda i: (0, i, 0, 0)),
                   pl.BlockSpec((N_KV_HEADS, TM, HEAD_DIM), lambda i: (0, i, 0)),
                   pl.BlockSpec((TM, SGU_W), lambda i: (i, 0)),
                   pl.BlockSpec((TM, 2 * FNET_W), lambda i: (i, 0))],
        out_shape=[jax.ShapeDtypeStruct((N_Q_HEADS, n, HEAD_DIM), BF16),
                   jax.ShapeDtypeStruct((N_KV_HEADS, n // LANES, HEAD_DIM, LANES), BF16),
                   jax.ShapeDtypeStruct((N_KV_HEADS, n, HEAD_DIM), BF16),
                   jax.ShapeDtypeStruct((n, SGU_W), BF16),
                   jax.ShapeDtypeStruct((n, 2 * FNET_W), BF16)],
        compiler_params=_params(("arbitrary",)),
    )(t, mods, norm1_g, w_in_bf, cos_t, sin_t, qg, kg, msq, rot, sgug, wcat, sbias, dftc)


def _attn_kernel(q_ref, kt_ref, v_ref, o_ref, m_sc, l_sc, acc_sc, *, n_lat_tiles, n_lat_chunks, ctx_len, lat_len):
    i = pl.program_id(1)
    rows = Q_PER_KV * TM
    q = q_ref[...].reshape(rows, HEAD_DIM)
    m_sc[...] = jnp.full((rows, 1), -jnp.inf, F32)
    l_sc[...] = jnp.zeros((rows, 1), F32)
    acc_sc[...] = jnp.zeros((rows, HEAD_DIM), F32)

    def step(kt, v):
        s = _dot(q, kt)
        m_old = m_sc[...]
        m_new = jnp.maximum(m_old, jnp.max(s, axis=-1, keepdims=True))
        alpha = jnp.exp(m_old - m_new)
        p = jnp.exp(s - m_new)
        l_sc[...] = alpha * l_sc[...] + jnp.sum(p, axis=-1, keepdims=True)
        acc_sc[...] = alpha * acc_sc[...] + _dot(p.astype(BF16), v)
        m_sc[...] = m_new

    sub = TK // LANES

    def body(j, carry):
        kt = jnp.concatenate([kt_ref[j * sub + c] for c in range(sub)], axis=1)
        v = v_ref[pl.ds(pl.multiple_of(j * TK, TK), TK), :]
        step(kt, v)
        return carry

    lax.fori_loop(0, jnp.where(i < n_lat_tiles, n_lat_chunks, 0), body, 0)
    cb = lat_len // LANES
    kt_c = jnp.concatenate([kt_ref[cb + c] for c in range(ctx_len // LANES)], axis=1)
    step(kt_c, v_ref[lat_len:lat_len + ctx_len, :])

    o = acc_sc[...] / l_sc[...]
    o_ref[...] = jnp.concatenate([o[g * TM:(g + 1) * TM] for g in range(Q_PER_KV)], axis=1).astype(BF16)


def _attention(q, kt, v, lat_len, ctx_len):
    n = q.shape[1]
    nt = n // TM
    kern = functools.partial(_attn_kernel, n_lat_tiles=lat_len // TM, n_lat_chunks=lat_len // TK,
                             ctx_len=ctx_len, lat_len=lat_len)
    rows = Q_PER_KV * TM
    return pl.pallas_call(
        kern,
        grid=(N_KV_HEADS, nt),
        in_specs=[pl.BlockSpec((Q_PER_KV, TM, HEAD_DIM), lambda h, i: (h, i, 0)),
                  pl.BlockSpec((None, n // LANES, HEAD_DIM, LANES), lambda h, i: (h, 0, 0, 0)),
                  pl.BlockSpec((None, n, HEAD_DIM), lambda h, i: (h, 0, 0))],
        out_specs=pl.BlockSpec((TM, Q_PER_KV * HEAD_DIM), lambda h, i: (i, h)),
        out_shape=jax.ShapeDtypeStruct((n, ATTN_W), BF16),
        scratch_shapes=[pltpu.VMEM((rows, 1), F32), pltpu.VMEM((rows, 1), F32),
                        pltpu.VMEM((rows, HEAD_DIM), F32)],
        compiler_params=_params(("arbitrary", "arbitrary")),
    )(q, kt, v)


FFT_BATCH = 4


def _fft1_kernel(g_ref, cs_ref, twc_ref, tws_ref, o_ref):
    n1 = g_ref.shape[0]
    w = 2 * FNET_W
    for b in range(FFT_BATCH):
        xg = g_ref[:, b * w:(b + 1) * w]
        pq = _dot(cs_ref[...], xg)
        ar = pq[:n1, :FNET_W] + pq[n1:, FNET_W:]
        ai = pq[:n1, FNET_W:] - pq[n1:, :FNET_W]
        c = jnp.concatenate([twc_ref[b]] * (FNET_W // LANES), axis=1)
        s = jnp.concatenate([tws_ref[b]] * (FNET_W // LANES), axis=1)
        o_ref[:, b * w:b * w + FNET_W] = (ar * c + ai * s).astype(BF16)
        o_ref[:, b * w + FNET_W:(b + 1) * w] = (ai * c - ar * s).astype(BF16)


def _fft2_kernel(t_ref, cs_ref, o_ref):
    for b in range(FFT_BATCH):
        tb = t_ref[b * LANES:(b + 1) * LANES, :]
        st = jnp.concatenate([tb[:, :FNET_W], tb[:, FNET_W:]], axis=0)
        o_ref[:, b * FNET_W:(b + 1) * FNET_W] = _dot(cs_ref[...], st).astype(BF16)


def _fftc_kernel(g_ref, cs_ref, o_ref):
    gb = g_ref[...]
    st = jnp.concatenate([gb[:, :FNET_W], gb[:, FNET_W:]], axis=0)
    o_ref[...] = _dot(cs_ref[...], st).astype(BF16)


def _fourier_positions(g, lat_len, ctx_len, consts):
    n1 = lat_len // LANES
    w = 2 * FNET_W
    g_lat = g[:lat_len].reshape(n1, LANES * w)
    t = pl.pallas_call(
        _fft1_kernel,
        grid=(LANES // FFT_BATCH,),
        in_specs=[pl.BlockSpec((n1, FFT_BATCH * w), lambda j: (0, j)),
                  pl.BlockSpec((2 * n1, n1), lambda j: (0, 0)),
                  pl.BlockSpec((FFT_BATCH, n1, LANES), lambda j: (j, 0, 0)),
                  pl.BlockSpec((FFT_BATCH, n1, LANES), lambda j: (j, 0, 0))],
        out_specs=pl.BlockSpec((n1, FFT_BATCH * w), lambda j: (0, j)),
        out_shape=jax.ShapeDtypeStruct((n1, LANES * w), BF16),
        compiler_params=_params(("arbitrary",)),
    )(g_lat, consts['cs1'], consts['twc'], consts['tws'])
    y_lat = pl.pallas_call(
        _fft2_kernel,
        grid=(n1 // FFT_BATCH,),
        in_specs=[pl.BlockSpec((FFT_BATCH * LANES, w), lambda j: (j, 0)),
                  pl.BlockSpec((LANES, 2 * LANES), lambda j: (0, 0))],
        out_specs=pl.BlockSpec((LANES, FFT_BATCH * FNET_W), lambda j: (0, j)),
        out_shape=jax.ShapeDtypeStruct((LANES, n1 * FNET_W), BF16),
        compiler_params=_params(("arbitrary",)),
    )(t.reshape(n1 * LANES, w), consts['cs2'])
    y_ctx = pl.pallas_call(
        _fftc_kernel,
        grid=(1,),
        in_specs=[pl.BlockSpec((ctx_len, w), lambda j: (lat_len // ctx_len, 0)),
                  pl.BlockSpec((ctx_len, 2 * ctx_len), lambda j: (0, 0))],
        out_specs=pl.BlockSpec((ctx_len, FNET_W), lambda j: (0, 0)),
        out_shape=jax.ShapeDtypeStruct((ctx_len, FNET_W), BF16),
        compiler_params=_params(("arbitrary",)),
    )(g, consts['csc'])
    return y_lat.reshape(lat_len, FNET_W), y_ctx


def _outproj_kernel(x_ref, mod_ref, o_ref, sgu_ref, yl_ref, yc_ref, wout_ref, g2_ref, rwt_ref,
                    xo_ref, h2_ref, aff_ref, *, n_lat_tiles):
    i = pl.program_id(0)
    yf = jnp.where(i < n_lat_tiles, yl_ref[...], yc_ref[...])
    mix = _dot(o_ref[...], wout_ref[:ATTN_W, :])
    mix = mix + _dot(sgu_ref[...], wout_ref[ATTN_W:ATTN_W + SGU_W, :])
    mix = mix + _dot(yf, wout_ref[ATTN_W + SGU_W:, :])
    x = x_ref[...] + mod_ref[2:3, :] * mix
    xo_ref[...] = x
    h = x * lax.rsqrt(jnp.mean(x * x, axis=-1, keepdims=True) + EPS) * g2_ref[...]
    h = h * (1.0 + mod_ref[4:5, :]) + mod_ref[3:4, :]
    h2_ref[...] = h
    logits = _dot_nt_x3(rwt_ref[...], h)
    z = jnp.exp(logits - jnp.max(logits, axis=0, keepdims=True))
    aff_ref[...] = z / jnp.sum(z, axis=0, keepdims=True)


def _outproj(layer, t, mods, o, sgu, y_lat, y_ctx, w_out_bf, norm2_g, rwt, nl):
    n = t.shape[0]
    nt = n // TM
    lay = lambda *shape: pl.BlockSpec((None,) + shape, lambda i: (layer,) + (0,) * len(shape))
    return pl.pallas_call(
        functools.partial(_outproj_kernel, n_lat_tiles=nl),
        grid=(nt,),
        in_specs=[pl.BlockSpec((TM, D_MODEL), lambda i: (i, 0)),
                  pl.BlockSpec((None, None, 6, D_MODEL), lambda i: (layer, i // nl, 0, 0)),
                  pl.BlockSpec((TM, ATTN_W), lambda i: (i, 0)),
                  pl.BlockSpec((TM, SGU_W), lambda i: (i, 0)),
                  pl.BlockSpec((TM, FNET_W), lambda i: (jnp.minimum(i, nl - 1), 0)),
                  pl.BlockSpec((TM, FNET_W), lambda i: (0, 0)),
                  lay(D_MODEL, D_MODEL), lay(1, D_MODEL), lay(N_EXPERTS, D_MODEL)],
        out_specs=[pl.BlockSpec((TM, D_MODEL), lambda i: (i, 0)),
                   pl.BlockSpec((TM, D_MODEL), lambda i: (i, 0)),
                   pl.BlockSpec((N_EXPERTS, TM), lambda i: (0, i))],
        out_shape=[jax.ShapeDtypeStruct((n, D_MODEL), F32),
                   jax.ShapeDtypeStruct((n, D_MODEL), F32),
                   jax.ShapeDtypeStruct((N_EXPERTS, n), F32)],
        compiler_params=_params(("arbitrary",)),
    )(t, mods, o, sgu, y_lat, y_ctx, w_out_bf, norm2_g, rwt)


def _route_kernel(a_ref, ut_ref, lt_ref, slt_ref, slot_ref, off_ref, idx_ref, *, rows, cap, capw):
    e_n = N_EXPERTS
    a = a_ref[...]
    bits = pltpu.bitcast(a, jnp.int32).reshape(e_n, rows, LANES)

    def count(mask):
        c = jnp.sum(jnp.where(mask, 1.0, 0.0), axis=1, keepdims=True)
        return jnp.sum(c, axis=2, keepdims=True)

    def search(it, thr):
        cand = thr | jnp.left_shift(jnp.int32(1), 30 - it)
        return jnp.where(count(bits >= cand) >= cap, cand, thr)

    thr = lax.fori_loop(0, 31, search, jnp.zeros((e_n, 1, 1), jnp.int32))
    gt = bits > thr
    eq = bits == thr
    need = cap - count(gt)

    def ranks(mask3):
        m = jnp.where(mask3, 1.0, 0.0).reshape(e_n * rows, LANES)
        cin = _dot(m.astype(BF16), ut_ref[...])
        tot = cin[:, LANES - 1:LANES]
        totb = jnp.broadcast_to(tot, (e_n * rows, LANES)).astype(BF16)
        offs = [_dot(slt_ref[...], totb[e * rows:(e + 1) * rows]) for e in range(e_n)]
        off = jnp.concatenate(offs, axis=0)
        return m, cin, off

    m_eq, cin_eq, off_eq = ranks(eq)
    eq_rank = (off_eq + cin_eq - m_eq).reshape(e_n, rows, LANES)
    sel = gt | (eq & (eq_rank < need))
    m_sel, cin_sel, off_sel = ranks(sel)
    pos = off_sel + cin_sel - m_sel
    slot_ref[...] = jnp.where(m_sel > 0.5, pos, -1.0).astype(jnp.int32)
    off_ref[...] = off_sel.astype(jnp.int32)

    r_lane = lax.broadcasted_iota(jnp.int32, (1, capw), 1).astype(F32)
    row_iota = lax.broadcasted_iota(jnp.int32, (rows, capw), 0).astype(F32)
    for e in range(e_n):
        sl = slice(e * rows, (e + 1) * rows)
        off_e = off_sel[sl, 0:1]
        end_e = off_e + cin_sel[sl, LANES - 1:LANES]
        a_of_r = jnp.sum(jnp.where(end_e <= r_lane, 1.0, 0.0), axis=0, keepdims=True)
        onehot = row_iota == a_of_r
        off_at = jnp.sum(jnp.where(onehot, off_e, 0.0), axis=0, keepdims=True)
        cin_t = _dot_nt(lt_ref[...], m_sel[sl].astype(BF16))
        crow = _dot(cin_t.astype(BF16), jnp.where(onehot, 1.0, 0.0).astype(BF16))
        b_of_r = jnp.sum(jnp.where(crow <= r_lane - off_at, 1.0, 0.0), axis=0, keepdims=True)
        idx_ref[e:e + 1, :] = (a_of_r * LANES + b_of_r).astype(jnp.int32)


def _route(aff2, rows, cap, ut, lt, slt):
    capw = max(cap, LANES)
    full = lambda *shape: pl.BlockSpec(shape, lambda i: (0,) * len(shape))
    return pl.pallas_call(
        functools.partial(_route_kernel, rows=rows, cap=cap, capw=capw),
        grid=(1,),
        in_specs=[full(N_EXPERTS * rows, LANES), full(LANES, LANES), full(LANES, LANES), full(rows, rows)],
        out_specs=[full(N_EXPERTS * rows, LANES), full(N_EXPERTS * rows, LANES), full(N_EXPERTS, capw)],
        out_shape=[jax.ShapeDtypeStruct((N_EXPERTS * rows, LANES), jnp.int32),
                   jax.ShapeDtypeStruct((N_EXPERTS * rows, LANES), jnp.int32),
                   jax.ShapeDtypeStruct((N_EXPERTS, capw), jnp.int32)],
        compiler_params=_params(("arbitrary",)),
    )(aff2, ut, lt, slt)


def _gather_kernel(idx_ref, h_hbm, o_ref, buf, sem, *, capt, rb):
    base = pl.program_id(0) * capt + pl.program_id(1) * rb

    def issue(r, carry):
        tok = idx_ref[base + r]
        pltpu.make_async_copy(h_hbm.at[pl.ds(tok, 1), :], buf.at[pl.ds(r, 1), :], sem).start()
        return carry

    lax.fori_loop(0, rb, issue, 0)
    pltpu.make_async_copy(h_hbm.at[pl.ds(0, rb), :], buf, sem).wait()
    o_ref[...] = buf[...].astype(BF16)


def _row_block(capt, limit=512):
    return max(r for r in range(WIN_ALIGN, limit + 1, WIN_ALIGN) if capt % r == 0)


def _gather(idx_flat, h2, capt):
    rb = _row_block(capt)
    return pl.pallas_call(
        functools.partial(_gather_kernel, capt=capt, rb=rb),
        grid_spec=pltpu.PrefetchScalarGridSpec(
            num_scalar_prefetch=1,
            grid=(N_EXPERTS, capt // rb),
            in_specs=[pl.BlockSpec(memory_space=pl.ANY)],
            out_specs=pl.BlockSpec((None, rb, D_MODEL), lambda e, r, idx: (e, r, 0)),
            scratch_shapes=[pltpu.VMEM((rb, D_MODEL), F32), pltpu.SemaphoreType.DMA(())]),
        out_shape=jax.ShapeDtypeStruct((N_EXPERTS, capt, D_MODEL), BF16),
        compiler_params=_params(("arbitrary", "arbitrary")),
    )(idx_flat, h2)


def _ffn_kernel(xs_ref, wg_ref, wu_ref, wd_ref, y_ref, acc_ref, *, rc):
    j = pl.program_id(1)
    wg = wg_ref[...].astype(BF16)
    wu = wu_ref[...].astype(BF16)
    wd = wd_ref[...].astype(BF16)
    capt = xs_ref.shape[0]
    for r in range(capt // rc):
        sl = slice(r * rc, (r + 1) * rc)
        xs = xs_ref[sl, :]
        a = _dot(xs, wg)
        u = _dot(xs, wu)
        hmid = (a * _sigmoid(a) * u).astype(BF16)
        part = _dot(hmid, wd)

        @pl.when(j == 0)
        def _():
            acc_ref[sl, :] = part

        @pl.when(j > 0)
        def _():
            acc_ref[sl, :] += part

    @pl.when(j == pl.num_programs(1) - 1)
    def _():
        y_ref[...] = acc_ref[...].astype(BF16)


def _ffn(layer, xs, w_gate, w_up, w_down):
    capt = xs.shape[1]
    rc = _row_block(capt)
    return pl.pallas_call(
        functools.partial(_ffn_kernel, rc=rc),
        grid=(N_EXPERTS, EXPERT_FF // FF_TILE),
        in_specs=[pl.BlockSpec((None, capt, D_MODEL), lambda e, j: (e, 0, 0)),
                  pl.BlockSpec((None, None, D_MODEL, FF_TILE), lambda e, j: (layer, e, 0, j)),
                  pl.BlockSpec((None, None, D_MODEL, FF_TILE), lambda e, j: (layer, e, 0, j)),
                  pl.BlockSpec((None, None, FF_TILE, D_MODEL), lambda e, j: (layer, e, j, 0))],
        out_specs=pl.BlockSpec((None, capt, D_MODEL), lambda e, j: (e, 0, 0)),
        out_shape=jax.ShapeDtypeStruct((N_EXPERTS, capt, D_MODEL), BF16),
        scratch_shapes=[pltpu.VMEM((capt, D_MODEL), F32)],
        compiler_params=_params(("arbitrary", "arbitrary")),
    )(xs, w_gate, w_up, w_down)


def _combine_kernel(start_ref, x_ref, mod_ref, slot_ref, aff_ref, fg_ref, y_hbm, o_ref, win, sem, *, final):
    i = pl.program_id(0)
    win_rows = win.shape[1]

    def copy(e, b):
        st = pl.multiple_of(start_ref[i * N_EXPERTS + e], WIN_ALIGN)
        return pltpu.make_async_copy(y_hbm.at[e, pl.ds(st, win_rows), :], win.at[b], sem.at[b])

    copy(0, 0).start()
    lane = lax.broadcasted_iota(jnp.int32, (TM, win_rows), 1)
    acc = jnp.zeros((TM, D_MODEL), F32)
    for e in range(N_EXPERTS):
        b = e % 2
        if e + 1 < N_EXPERTS:
            copy(e + 1, 1 - b).start()
        copy(e, b).wait()
        rel = slot_ref[:, e:e + 1] - start_ref[i * N_EXPERTS + e]
        onehot = jnp.where(rel == lane, aff_ref[:, e:e + 1], 0.0).astype(BF16)
        acc = acc + _dot(onehot, win[b])
    x = x_ref[...] + mod_ref[5:6, :] * acc
    if final:
        x = x * lax.rsqrt(jnp.mean(x * x, axis=-1, keepdims=True) + EPS) * fg_ref[...]
    o_ref[...] = x


def _combine(layer, starts, t, mods, slot_t, aff_t, final_g, y, nl, win_rows, final):
    n = t.shape[0]
    nt = nl if final else n // TM
    return pl.pallas_call(
        functools.partial(_combine_kernel, final=final),
        grid_spec=pltpu.PrefetchScalarGridSpec(
            num_scalar_prefetch=1,
            grid=(nt,),
            in_specs=[pl.BlockSpec((TM, D_MODEL), lambda i, s: (i, 0)),
                      pl.BlockSpec((None, None, 6, D_MODEL), lambda i, s: (layer, i // nl, 0, 0)),
                      pl.BlockSpec((TM, N_EXPERTS), lambda i, s: (i, 0)),
                      pl.BlockSpec((TM, N_EXPERTS), lambda i, s: (i, 0)),
                      pl.BlockSpec((1, D_MODEL), lambda i, s: (0, 0)),
                      pl.BlockSpec(memory_space=pl.ANY)],
            out_specs=pl.BlockSpec((TM, D_MODEL), lambda i, s: (i, 0)),
            scratch_shapes=[pltpu.VMEM((2, win_rows, D_MODEL), BF16), pltpu.SemaphoreType.DMA((2,))]),
        out_shape=jax.ShapeDtypeStruct((nt * TM, D_MODEL), F32),
        compiler_params=_params(("arbitrary",)),
    )(starts, t, mods, slot_t, aff_t, final_g, y)


def _constants(lat_len, ctx_len):
    n1 = lat_len // LANES
    c64, s64 = _dft_cs(FNET_GROUP_W)
    groups = FNET_W // FNET_GROUP_W
    dftc = np.concatenate([_blockdiag(c64, groups), -_blockdiag(s64, groups)], axis=1) / math.sqrt(FNET_GROUP_W)
    c1, s1 = _dft_cs(n1)
    cs1 = np.concatenate([c1, s1], axis=0) / math.sqrt(n1)
    c2, s2 = _dft_cs(LANES)
    cs2 = np.concatenate([c2, s2], axis=1) / math.sqrt(LANES)
    cc, sc = _dft_cs(ctx_len)
    csc = np.concatenate([cc, sc], axis=1) / math.sqrt(ctx_len)
    n2 = jnp.arange(LANES, dtype=F32)[:, None]
    k1 = jnp.arange(n1, dtype=F32)[None, :]
    theta = (n2 * k1) * (2.0 * math.pi / lat_len)
    bcast = lambda a: jnp.broadcast_to(a[:, :, None], (LANES, n1, LANES))
    tri = np.triu(np.ones((LANES, LANES)))
    return dict(
        dftc=jnp.asarray(dftc, BF16), cs1=jnp.asarray(cs1, BF16), cs2=jnp.asarray(cs2, BF16),
        csc=jnp.asarray(csc, BF16), twc=bcast(jnp.cos(theta)), tws=bcast(jnp.sin(theta)),
        msq=jnp.asarray(_blockdiag(np.full((HEAD_DIM, HEAD_DIM), 1.0 / HEAD_DIM), N_Q_HEADS), BF16),
        rot=jnp.asarray(_rot_matrix(ATTN_W), BF16),
        ut=jnp.asarray(tri, BF16), lt=jnp.asarray(tri.T, BF16),
        slt_lat=jnp.asarray(np.tril(np.ones((n1, n1)), -1), BF16),
        slt_ctx=jnp.asarray(np.tril(np.ones((CTX_ROUTE_ROWS, CTX_ROUTE_ROWS)), -1), BF16),
    )


def kernel(x, c, ctx, c_ctx, w_ada, b_ada, norm1_g, w_in, q_norm_g, k_norm_g, sgu_norm_g, sgu_w, sgu_b,
           w_out, norm2_g, router_w, w_gate, w_up, w_down, final_g):
    assert x.shape[0] == 1 and ctx.shape[0] == 1
    lat_len, ctx_len = x.shape[1], ctx.shape[1]
    assert ctx_len == TM and lat_len % TK == 0 and lat_len % (LANES * FFT_BATCH) == 0
    n = lat_len + ctx_len
    nl = lat_len // TM
    n1 = lat_len // LANES
    cap_l = CAPACITY_FACTOR * lat_len // N_EXPERTS
    cap_c = CAPACITY_FACTOR * ctx_len // N_EXPERTS
    capt = cap_l + cap_c
    assert cap_c <= LANES and capt % WIN_ALIGN == 0
    win_rows = min(COMBINE_WIN, capt)
    depth = w_ada.shape[0]
    consts = _constants(lat_len, ctx_len)
    cos_t, sin_t = _rope_tables(lat_len, ctx_len)

    cvec = jnp.zeros((16, D_MODEL), F32).at[0].set(c[0]).at[1].set(c_ctx)
    mods = _mods(cvec, w_ada, b_ada)[:, :2].reshape(depth, 2, 6, D_MODEL)

    w_in_bf = w_in.astype(BF16)
    w_out_bf = w_out.astype(BF16)
    qg = jnp.tile(q_norm_g, (1, N_Q_HEADS)).reshape(depth, 1, ATTN_W)
    kg = jnp.tile(k_norm_g, (1, N_KV_HEADS)).reshape(depth, 1, KV_W)
    wcat = jnp.transpose(sgu_w, (0, 2, 1, 3)).reshape(depth, CHUNK, SGU_GROUPS * CHUNK).astype(BF16)
    sbias = jnp.repeat(jnp.transpose(sgu_b, (0, 2, 1)), SGU_GROUP_W, axis=2)
    rwt = jnp.transpose(router_w, (0, 2, 1))
    g1 = norm1_g.reshape(depth, 1, D_MODEL)
    g2 = norm2_g.reshape(depth, 1, D_MODEL)
    sgug = sgu_norm_g.reshape(depth, 1, SGU_W)
    fg = final_g.reshape(1, D_MODEL)

    t = jnp.concatenate([x[0], ctx[0]], axis=0)
    for layer in range(depth):
        final = layer == depth - 1
        q, kt, v, sgu, g = _proj(layer, t, mods, g1, w_in_bf, cos_t, sin_t, qg, kg, consts['msq'], consts['rot'],
                                 sgug, wcat, sbias, consts['dftc'], nl)
        o = _attention(q, kt, v, lat_len, ctx_len)
        y_lat, y_ctx = _fourier_positions(g, lat_len, ctx_len, consts)
        t, h2, aff = _outproj(layer, t, mods, o, sgu, y_lat, y_ctx, w_out_bf, g2, rwt, nl)

        aff_lat = aff[:, :lat_len].reshape(N_EXPERTS * n1, LANES)
        aff_ctx = jnp.pad(aff[:, lat_len:].reshape(N_EXPERTS, ctx_len // LANES, LANES),
                          ((0, 0), (0, CTX_ROUTE_ROWS - ctx_len // LANES), (0, 0))
                          ).reshape(N_EXPERTS * CTX_ROUTE_ROWS, LANES)
        slot_l, off_l, idx_l = _route(aff_lat, n1, cap_l, consts['ut'], consts['lt'], consts['slt_lat'])
        slot_c, _, idx_c = _route(aff_ctx, CTX_ROUTE_ROWS, cap_c, consts['ut'], consts['lt'], consts['slt_ctx'])

        idx = jnp.concatenate([idx_l[:, :cap_l], idx_c[:, :cap_c] + lat_len], axis=1)
        slot_c = slot_c.reshape(N_EXPERTS, CTX_ROUTE_ROWS * LANES)[:, :ctx_len]
        slot = jnp.concatenate([slot_l.reshape(N_EXPERTS, lat_len),
                                jnp.where(slot_c >= 0, slot_c + cap_l, -1)], axis=1)
        lo = off_l.reshape(N_EXPERTS, n1, LANES)[:, ::TM // LANES, 0]
        starts = jnp.minimum((lo // WIN_ALIGN) * WIN_ALIGN, capt - win_rows)
        starts = jnp.concatenate([starts, jnp.full((N_EXPERTS, 1), capt - win_rows, jnp.int32)], axis=1)

        xs = _gather(idx.reshape(-1), h2, capt)
        y = _ffn(layer, xs, w_gate, w_up, w_down)
        t = _combine(layer, starts.T.reshape(-1), t, mods, slot.T, aff.T, fg, y, nl, win_rows, final)
    return t.reshape(1, lat_len, D_MODEL)
```

```python
import functools
import math

import numpy as np
import jax
import jax.numpy as jnp
from jax import lax
from jax.experimental import pallas as pl
from jax.experimental.pallas import tpu as pltpu

D_MODEL = 1024
DEPTH = 4
GRID_W = 64
HEAD_DIM = 64
ATTN_W = 512
N_Q_HEADS = 8
N_KV_HEADS = 2
Q_PER_KV = 4
KV_W = 128
AXIS_DIM = 32
ROPE_BASE = 10000.0
SGU_W = 256
SGU_GROUP_W = 64
SGU_GROUPS = 4
CHUNK = 128
FNET_W = 256
FNET_GROUP_W = 64
IN_W = 1536
N_EXPERTS = 16
CAPACITY_FACTOR = 2
EXPERT_FF = 2048
EPS = 1e-6

LANES = 128
TM = 256
TK = 2048
FF_TILE = 512
COMBINE_WIN = 384
WIN_ALIGN = 16
CTX_ROUTE_ROWS = 16
V_ROWS = HEAD_DIM + 16
LOG2E = math.log2(math.e)
VMEM_LIMIT = 56 * 1024 * 1024

F32 = jnp.float32
BF16 = jnp.bfloat16


def _dot(a, b):
    return jnp.dot(a, b, preferred_element_type=F32)


def _dot_nt(a, b):
    return lax.dot_general(a, b, (((1,), (1,)), ((), ())), preferred_element_type=F32)


def _split(a):
    hi = a.astype(BF16)
    lo = (a - hi.astype(F32)).astype(BF16)
    return hi, lo


def _dot_x2(a, b_bf16):
    hi, lo = _split(a)
    return _dot(hi, b_bf16) + _dot(lo, b_bf16)


def _dot_x3(a, b):
    ah, al = _split(a)
    bh, bl = _split(b)
    return _dot(ah, bh) + (_dot(al, bh) + _dot(ah, bl))


def _dot_nt_x3(a, b):
    ah, al = _split(a)
    bh, bl = _split(b)
    return _dot_nt(ah, bh) + (_dot_nt(al, bh) + _dot_nt(ah, bl))


def _sigmoid(a):
    return 1.0 / (1.0 + jnp.exp(-a))


def _params(sem, vmem=VMEM_LIMIT):
    return pltpu.CompilerParams(dimension_semantics=sem, vmem_limit_bytes=vmem)


def _blockdiag(block, n):
    k = block.shape[0]
    out = np.zeros((k * n, block.shape[1] * n), np.float64)
    for i in range(n):
        out[i * k:(i + 1) * k, i * block.shape[1]:(i + 1) * block.shape[1]] = block
    return out


def _dft_cs(n):
    a = 2.0 * np.pi * np.outer(np.arange(n), np.arange(n)) / n
    return np.cos(a), np.sin(a)


def _rot_matrix(width):
    h = AXIS_DIM // 2
    r = np.zeros((width, width), np.float64)
    for j in range(width):
        if (j % AXIS_DIM) < h:
            r[j + h, j] = -1.0
        else:
            r[j - h, j] = 1.0
    return r


def _rope_tables(L, C):
    rows = L // GRID_W
    r, col = jnp.meshgrid(jnp.arange(rows, dtype=F32), jnp.arange(GRID_W, dtype=F32), indexing='ij')
    inv = 1.0 / (ROPE_BASE ** (jnp.arange(0, AXIS_DIM, 2, dtype=F32) / AXIS_DIM))
    ar = r.reshape(-1, 1) * inv
    ac = col.reshape(-1, 1) * inv
    cos = jnp.concatenate([jnp.cos(ar), jnp.cos(ar), jnp.cos(ac), jnp.cos(ac)], axis=-1)
    sin = jnp.concatenate([jnp.sin(ar), jnp.sin(ar), jnp.sin(ac), jnp.sin(ac)], axis=-1)
    cos = jnp.concatenate([cos, jnp.ones((C, HEAD_DIM), F32)], axis=0)
    sin = jnp.concatenate([sin, jnp.zeros((C, HEAD_DIM), F32)], axis=0)
    return jnp.tile(cos, (1, 2)), jnp.tile(sin, (1, 2))


def _mods_kernel(cv_ref, w_ref, b_ref, o_ref):
    cv = cv_ref[...]
    s = cv * _sigmoid(cv)
    o_ref[...] = _dot_x3(s, w_ref[...]) + b_ref[...]


def _mods(cvec, w_ada, b_ada):
    depth, d, six_d = w_ada.shape
    rows = cvec.shape[0]
    tn = 1024
    return pl.pallas_call(
        _mods_kernel,
        grid=(depth, six_d // tn),
        in_specs=[pl.BlockSpec((rows, d), lambda l, j: (0, 0)),
                  pl.BlockSpec((None, d, tn), lambda l, j: (l, 0, j)),
                  pl.BlockSpec((None, 1, tn), lambda l, j: (l, 0, j))],
        out_specs=pl.BlockSpec((None, rows, tn), lambda l, j: (l, 0, j)),
        out_shape=jax.ShapeDtypeStruct((depth, rows, six_d), F32),
        compiler_params=_params(("arbitrary", "arbitrary")),
        name="adaln_mods",
    )(cvec, w_ada, b_ada.reshape(depth, 1, six_d))


def _head_norm_rope(t, gain, cos, sin, msq, rot):
    ms = _dot_x2(t * t, msq)
    tn = t * lax.rsqrt(ms + EPS) * gain
    return tn * cos + _dot(tn.astype(BF16), rot) * sin


def _gelu_tanh(z):
    return 0.5 * z * (1.0 + jnp.tanh(math.sqrt(2.0 / math.pi) * (z + 0.044715 * (z * z * z))))


def _proj_kernel(x_ref, mod_ref, g1_ref, win_ref, cos_ref, sin_ref, qg_ref, kg_ref, msq_ref, rot_ref,
                 sgug_ref, wcat_ref, sbias_ref, dft_ref,
                 qt_ref, k_ref, vt_ref, sgu_ref, g_ref):
    x = x_ref[...]
    shift = mod_ref[0:1, :]
    scale = mod_ref[1:2, :]
    h = x * lax.rsqrt(jnp.mean(x * x, axis=-1, keepdims=True) + EPS) * g1_ref[...]
    h = h * (1.0 + scale) + shift
    p = _dot(h.astype(BF16), win_ref[...])

    cos2 = cos_ref[...]
    sin2 = sin_ref[...]
    cos8 = jnp.concatenate([cos2] * 4, axis=1)
    sin8 = jnp.concatenate([sin2] * 4, axis=1)
    qo = _head_norm_rope(p[:, :ATTN_W], qg_ref[...], cos8, sin8, msq_ref[...], rot_ref[...])
    qot = (qo * (HEAD_DIM ** -0.5 * LOG2E)).T
    for hh in range(N_Q_HEADS):
        qt_ref[hh] = qot[hh * HEAD_DIM:(hh + 1) * HEAD_DIM, :].astype(BF16)

    ko = _head_norm_rope(p[:, ATTN_W:ATTN_W + KV_W], kg_ref[...], cos2, sin2,
                         msq_ref[:KV_W, :KV_W], rot_ref[:KV_W, :KV_W])
    vt = p[:, ATTN_W + KV_W:ATTN_W + 2 * KV_W].T
    ones_rows = jnp.where(lax.broadcasted_iota(jnp.int32, (V_ROWS - HEAD_DIM, LANES), 0) == 0, 1.0, 0.0)
    for hh in range(N_KV_HEADS):
        k_ref[hh] = ko[:, hh * HEAD_DIM:(hh + 1) * HEAD_DIM].astype(BF16)
        for cc in range(TM // LANES):
            vt_ref[hh, cc, :HEAD_DIM, :] = vt[hh * HEAD_DIM:(hh + 1) * HEAD_DIM,
                                              cc * LANES:(cc + 1) * LANES].astype(BF16)
            vt_ref[hh, cc, HEAD_DIM:, :] = ones_rows.astype(BF16)

    o3 = ATTN_W + 2 * KV_W
    ge = _gelu_tanh(p[:, o3:o3 + 2 * SGU_W])
    u = ge[:, :SGU_W]
    vv = ge[:, SGU_W:]
    vms = _dot_x2(vv * vv, msq_ref[:SGU_W, :SGU_W])
    vn = vv * lax.rsqrt(vms + EPS) * sgug_ref[...]
    lane_group = lax.broadcasted_iota(jnp.int32, (CHUNK, SGU_W), 1) // SGU_GROUP_W
    for cc in range(TM // CHUNK):
        vc = vn[cc * CHUNK:(cc + 1) * CHUNK]
        vstack = jnp.concatenate([jnp.where(lane_group == gg, vc, 0.0) for gg in range(SGU_GROUPS)],
                                 axis=0).astype(BF16)
        s = _dot(wcat_ref[...], vstack) + sbias_ref[...]
        sgu_ref[cc * CHUNK:(cc + 1) * CHUNK, :] = (u[cc * CHUNK:(cc + 1) * CHUNK] * s).astype(BF16)

    o4 = o3 + 2 * SGU_W
    g_ref[...] = _dot(p[:, o4:].astype(BF16), dft_ref[...]).astype(BF16)


def _proj(layer, t, mods, norm1_g, w_in_bf, cos_t, sin_t, qg, kg, msq, rot, sgug, wcat, sbias, dftc, nl):
    n = t.shape[0]
    nt = n // TM
    const = lambda *shape: pl.BlockSpec(shape, lambda i: (0,) * len(shape))
    lay = lambda *shape: pl.BlockSpec((None,) + shape, lambda i: (layer,) + (0,) * len(shape))
    return pl.pallas_call(
        _proj_kernel,
        grid=(nt,),
        in_specs=[pl.BlockSpec((TM, D_MODEL), lambda i: (i, 0)),
                  pl.BlockSpec((None, None, 6, D_MODEL), lambda i: (layer, i // nl, 0, 0)),
                  lay(1, D_MODEL),
                  lay(D_MODEL, IN_W),
                  pl.BlockSpec((TM, 2 * HEAD_DIM), lambda i: (i, 0)),
                  pl.BlockSpec((TM, 2 * HEAD_DIM), lambda i: (i, 0)),
                  lay(1, ATTN_W), lay(1, KV_W),
                  const(ATTN_W, ATTN_W), const(ATTN_W, ATTN_W),
                  lay(1, SGU_W), lay(CHUNK, SGU_GROUPS * CHUNK), lay(CHUNK, SGU_W),
                  const(FNET_W, 2 * FNET_W)],
        out_specs=[pl.BlockSpec((N_Q_HEADS, HEAD_DIM, TM), lambda i: (0, 0, i)),
                   pl.BlockSpec((N_KV_HEADS, TM, HEAD_DIM), lambda i: (0, i, 0)),
                   pl.BlockSpec((N_KV_HEADS, TM // LANES, V_ROWS, LANES), lambda i: (0, i, 0, 0)),
                   pl.BlockSpec((TM, SGU_W), lambda i: (i, 0)),
                   pl.BlockSpec((TM, 2 * FNET_W), lambda i: (i, 0))],
        out_shape=[jax.ShapeDtypeStruct((N_Q_HEADS, HEAD_DIM, n), BF16),
                   jax.ShapeDtypeStruct((N_KV_HEADS, n, HEAD_DIM), BF16),
                   jax.ShapeDtypeStruct((N_KV_HEADS, n // LANES, V_ROWS, LANES), BF16),
                   jax.ShapeDtypeStruct((n, SGU_W), BF16),
                   jax.ShapeDtypeStruct((n, 2 * FNET_W), BF16)],
        compiler_params=_params(("arbitrary",)),
        name="proj",
    )(t, mods, norm1_g, w_in_bf, cos_t, sin_t, qg, kg, msq, rot, sgug, wcat, sbias, dftc)


def _attn_kernel(qt_ref, k_ref, vt_ref, o_ref, *st_sc, n_lat_tiles, n_lat_chunks, ctx_len, lat_len):
    i = pl.program_id(1)

    def step(k, vt, carry):
        out = []
        nk = k.shape[0]
        for g in range(Q_PER_KV):
            st_sc[g][:nk, :] = _dot(k, qt_ref[g])
        for g in range(Q_PER_KV):
            m_old, acc = carry[g]
            st = st_sc[g][:nk, :]
            m_new = jnp.maximum(m_old, jnp.max(st, axis=0, keepdims=True))
            alpha = jnp.exp2(m_old - m_new)
            pt = jnp.exp2(st - m_new).astype(BF16)
            out.append((m_new, alpha * acc + _dot(vt, pt)))
        return tuple(out)

    sub = TK // LANES

    def body(j, carry):
        k = k_ref[pl.ds(pl.multiple_of(j * TK, TK), TK), :]
        vt = jnp.concatenate([vt_ref[j * sub + c] for c in range(sub)], axis=1)
        return step(k, vt, carry)

    init = tuple((jnp.full((1, TM), -jnp.inf, F32), jnp.zeros((V_ROWS, TM), F32)) for _ in range(Q_PER_KV))
    carry = lax.fori_loop(0, jnp.where(i < n_lat_tiles, n_lat_chunks, 0), body, init)
    cb = lat_len // LANES
    vt_c = jnp.concatenate([vt_ref[cb + c] for c in range(ctx_len // LANES)], axis=1)
    carry = step(k_ref[lat_len:lat_len + ctx_len, :], vt_c, carry)

    o2 = jnp.concatenate([acc[:HEAD_DIM, :] / acc[HEAD_DIM:HEAD_DIM + 1, :] for _, acc in carry],
                         axis=0)
    o_ref[...] = o2.T.astype(BF16)


def _attention(qt, k, vt, lat_len, ctx_len):
    n = k.shape[1]
    nt = n // TM
    kern = functools.partial(_attn_kernel, n_lat_tiles=lat_len // TM, n_lat_chunks=lat_len // TK,
                             ctx_len=ctx_len, lat_len=lat_len)
    return pl.pallas_call(
        kern,
        grid=(N_KV_HEADS, nt),
        in_specs=[pl.BlockSpec((Q_PER_KV, HEAD_DIM, TM), lambda h, i: (h, 0, i)),
                  pl.BlockSpec((None, n, HEAD_DIM), lambda h, i: (h, 0, 0)),
                  pl.BlockSpec((None, n // LANES, V_ROWS, LANES), lambda h, i: (h, 0, 0, 0))],
        out_specs=pl.BlockSpec((TM, Q_PER_KV * HEAD_DIM), lambda h, i: (i, h)),
        out_shape=jax.ShapeDtypeStruct((n, ATTN_W), BF16),
        scratch_shapes=[pltpu.VMEM((TK, TM), F32) for _ in range(Q_PER_KV)],
        compiler_params=_params(("arbitrary", "arbitrary")),
        name="attention",
    )(qt, k, vt)


FFT_BATCH = 4


def _fft1_kernel(g_ref, cs_ref, twc_ref, tws_ref, o_ref):
    n1 = g_ref.shape[0]
    w = 2 * FNET_W
    for b in range(FFT_BATCH):
        xg = g_ref[:, b * w:(b + 1) * w]
        pq = _dot(cs_ref[...], xg)
        ar = pq[:n1, :FNET_W] + pq[n1:, FNET_W:]
        ai = pq[:n1, FNET_W:] - pq[n1:, :FNET_W]
        c = jnp.concatenate([twc_ref[b]] * (FNET_W // LANES), axis=1)
        s = jnp.concatenate([tws_ref[b]] * (FNET_W // LANES), axis=1)
        o_ref[:, b * w:b * w + FNET_W] = (ar * c + ai * s).astype(BF16)
        o_ref[:, b * w + FNET_W:(b + 1) * w] = (ai * c - ar * s).astype(BF16)


def _fft2_kernel(t_ref, cs_ref, o_ref):
    for b in range(FFT_BATCH):
        tb = t_ref[b * LANES:(b + 1) * LANES, :]
        st = jnp.concatenate([tb[:, :FNET_W], tb[:, FNET_W:]], axis=0)
        o_ref[:, b * FNET_W:(b + 1) * FNET_W] = _dot(cs_ref[...], st).astype(BF16)


def _fftc_kernel(g_ref, cs_ref, o_ref):
    gb = g_ref[...]
    st = jnp.concatenate([gb[:, :FNET_W], gb[:, FNET_W:]], axis=0)
    o_ref[...] = _dot(cs_ref[...], st).astype(BF16)


def _fourier_positions(g, lat_len, ctx_len, consts):
    n1 = lat_len // LANES
    w = 2 * FNET_W
    g_lat = g[:lat_len].reshape(n1, LANES * w)
    t = pl.pallas_call(
        _fft1_kernel,
        grid=(LANES // FFT_BATCH,),
        in_specs=[pl.BlockSpec((n1, FFT_BATCH * w), lambda j: (0, j)),
                  pl.BlockSpec((2 * n1, n1), lambda j: (0, 0)),
                  pl.BlockSpec((FFT_BATCH, n1, LANES), lambda j: (j, 0, 0)),
                  pl.BlockSpec((FFT_BATCH, n1, LANES), lambda j: (j, 0, 0))],
        out_specs=pl.BlockSpec((n1, FFT_BATCH * w), lambda j: (0, j)),
        out_shape=jax.ShapeDtypeStruct((n1, LANES * w), BF16),
        compiler_params=_params(("arbitrary",)),
        name="fft_stage1",
    )(g_lat, consts['cs1'], consts['twc'], consts['tws'])
    y_lat = pl.pallas_call(
        _fft2_kernel,
        grid=(n1 // FFT_BATCH,),
        in_specs=[pl.BlockSpec((FFT_BATCH * LANES, w), lambda j: (j, 0)),
                  pl.BlockSpec((LANES, 2 * LANES), lambda j: (0, 0))],
        out_specs=pl.BlockSpec((LANES, FFT_BATCH * FNET_W), lambda j: (0, j)),
        out_shape=jax.ShapeDtypeStruct((LANES, n1 * FNET_W), BF16),
        compiler_params=_params(("arbitrary",)),
        name="fft_stage2",
    )(t.reshape(n1 * LANES, w), consts['cs2'])
    y_ctx = pl.pallas_call(
        _fftc_kernel,
        grid=(1,),
        in_specs=[pl.BlockSpec((ctx_len, w), lambda j: (lat_len // ctx_len, 0)),
                  pl.BlockSpec((ctx_len, 2 * ctx_len), lambda j: (0, 0))],
        out_specs=pl.BlockSpec((ctx_len, FNET_W), lambda j: (0, 0)),
        out_shape=jax.ShapeDtypeStruct((ctx_len, FNET_W), BF16),
        compiler_params=_params(("arbitrary",)),
        name="fft_ctx",
    )(g, consts['csc'])
    return y_lat.reshape(lat_len, FNET_W), y_ctx


def _outproj_kernel(x_ref, mod_ref, o_ref, sgu_ref, yl_ref, yc_ref, wout_ref, g2_ref, rwt_ref,
                    xo_ref, h2_ref, aff_ref, *, n_lat_tiles):
    i = pl.program_id(0)
    yf = jnp.where(i < n_lat_tiles, yl_ref[...], yc_ref[...])
    mix = _dot(o_ref[...], wout_ref[:ATTN_W, :])
    mix = mix + _dot(sgu_ref[...], wout_ref[ATTN_W:ATTN_W + SGU_W, :])
    mix = mix + _dot(yf, wout_ref[ATTN_W + SGU_W:, :])
    x = x_ref[...] + mod_ref[2:3, :] * mix
    xo_ref[...] = x
    h = x * lax.rsqrt(jnp.mean(x * x, axis=-1, keepdims=True) + EPS) * g2_ref[...]
    h = h * (1.0 + mod_ref[4:5, :]) + mod_ref[3:4, :]
    h2_ref[...] = h
    logits = _dot_nt_x3(rwt_ref[...], h)
    z = jnp.exp(logits - jnp.max(logits, axis=0, keepdims=True))
    aff_ref[...] = z / jnp.sum(z, axis=0, keepdims=True)


def _outproj(layer, t, mods, o, sgu, y_lat, y_ctx, w_out_bf, norm2_g, rwt, nl):
    n = t.shape[0]
    nt = n // TM
    lay = lambda *shape: pl.BlockSpec((None,) + shape, lambda i: (layer,) + (0,) * len(shape))
    return pl.pallas_call(
        functools.partial(_outproj_kernel, n_lat_tiles=nl),
        grid=(nt,),
        in_specs=[pl.BlockSpec((TM, D_MODEL), lambda i: (i, 0)),
                  pl.BlockSpec((None, None, 6, D_MODEL), lambda i: (layer, i // nl, 0, 0)),
                  pl.BlockSpec((TM, ATTN_W), lambda i: (i, 0)),
                  pl.BlockSpec((TM, SGU_W), lambda i: (i, 0)),
                  pl.BlockSpec((TM, FNET_W), lambda i: (jnp.minimum(i, nl - 1), 0)),
                  pl.BlockSpec((TM, FNET_W), lambda i: (0, 0)),
                  lay(D_MODEL, D_MODEL), lay(1, D_MODEL), lay(N_EXPERTS, D_MODEL)],
        out_specs=[pl.BlockSpec((TM, D_MODEL), lambda i: (i, 0)),
                   pl.BlockSpec((TM, D_MODEL), lambda i: (i, 0)),
                   pl.BlockSpec((N_EXPERTS, TM), lambda i: (0, i))],
        out_shape=[jax.ShapeDtypeStruct((n, D_MODEL), F32),
                   jax.ShapeDtypeStruct((n, D_MODEL), F32),
                   jax.ShapeDtypeStruct((N_EXPERTS, n), F32)],
        compiler_params=_params(("arbitrary",)),
        name="outproj_router",
    )(t, mods, o, sgu, y_lat, y_ctx, w_out_bf, norm2_g, rwt)


def _route_kernel(a_ref, ut_ref, lt_ref, slt_ref, slot_ref, off_ref, idx_ref, *, rows, cap, capw):
    e_n = N_EXPERTS
    a3 = a_ref[...].reshape(e_n, rows, LANES)

    def count(mask):
        c = jnp.sum(jnp.where(mask, 1.0, 0.0), axis=1, keepdims=True)
        return jnp.sum(c, axis=2, keepdims=True)

    def as_float(bits):
        return lax.bitcast_convert_type(bits, F32)

    def search(it, thr):
        cand = thr | jnp.left_shift(jnp.int32(1), 30 - it)
        return jnp.where(count(a3 >= as_float(cand)) >= cap, cand, thr)

    thr = lax.fori_loop(0, 31, search, jnp.zeros((e_n, 1, 1), jnp.int32))
    above = as_float(thr + 1)
    gt = a3 >= above
    eq = jnp.logical_and(a3 >= as_float(thr), a3 < above)
    need = cap - count(gt)

    def ranks(mask3):
        m = jnp.where(mask3, 1.0, 0.0).reshape(e_n * rows, LANES)
        cin = _dot(m.astype(BF16), ut_ref[...])
        tot = cin[:, LANES - 1:LANES]
        totb = jnp.broadcast_to(tot, (e_n * rows, LANES)).astype(BF16)
        offs = [_dot(slt_ref[...], totb[e * rows:(e + 1) * rows]) for e in range(e_n)]
        off = jnp.concatenate(offs, axis=0)
        return m, cin, off

    m_eq, cin_eq, off_eq = ranks(eq)
    eq_rank = (off_eq + cin_eq - m_eq).reshape(e_n, rows, LANES)
    sel = gt | (eq & (eq_rank < need))
    m_sel, cin_sel, off_sel = ranks(sel)
    pos = off_sel + cin_sel - m_sel
    slot_ref[...] = jnp.where(m_sel > 0.5, pos, -1.0).astype(jnp.int32)
    off_ref[...] = off_sel.astype(jnp.int32)

    r_lane = lax.broadcasted_iota(jnp.int32, (1, capw), 1).astype(F32)
    row_iota = lax.broadcasted_iota(jnp.int32, (rows, capw), 0).astype(F32)
    for e in range(e_n):
        sl = slice(e * rows, (e + 1) * rows)
        off_e = off_sel[sl, 0:1]
        end_e = off_e + cin_sel[sl, LANES - 1:LANES]
        a_of_r = jnp.sum(jnp.where(end_e <= r_lane, 1.0, 0.0), axis=0, keepdims=True)
        onehot = row_iota == a_of_r
        off_at = jnp.sum(jnp.where(onehot, off_e, 0.0), axis=0, keepdims=True)
        cin_t = _dot_nt(lt_ref[...], m_sel[sl].astype(BF16))
        crow = _dot(cin_t.astype(BF16), jnp.where(onehot, 1.0, 0.0).astype(BF16))
        b_of_r = jnp.sum(jnp.where(crow <= r_lane - off_at, 1.0, 0.0), axis=0, keepdims=True)
        idx_ref[e:e + 1, :] = (a_of_r * LANES + b_of_r).astype(jnp.int32)


def _route(aff2, rows, cap, ut, lt, slt):
    capw = max(cap, LANES)
    full = lambda *shape: pl.BlockSpec(shape, lambda i: (0,) * len(shape))
    return pl.pallas_call(
        functools.partial(_route_kernel, rows=rows, cap=cap, capw=capw),
        grid=(1,),
        in_specs=[full(N_EXPERTS * rows, LANES), full(LANES, LANES), full(LANES, LANES), full(rows, rows)],
        out_specs=[full(N_EXPERTS * rows, LANES), full(N_EXPERTS * rows, LANES), full(N_EXPERTS, capw)],
        out_shape=[jax.ShapeDtypeStruct((N_EXPERTS * rows, LANES), jnp.int32),
                   jax.ShapeDtypeStruct((N_EXPERTS * rows, LANES), jnp.int32),
                   jax.ShapeDtypeStruct((N_EXPERTS, capw), jnp.int32)],
        compiler_params=_params(("arbitrary",)),
        name="route",
    )(aff2, ut, lt, slt)


def _gather_kernel(idx_ref, h_hbm, o_ref, buf, sem, *, capt, rb):
    base = pl.program_id(0) * capt + pl.program_id(1) * rb

    def issue(r, carry):
        tok = idx_ref[base + r]
        pltpu.make_async_copy(h_hbm.at[pl.ds(tok, 1), :], buf.at[pl.ds(r, 1), :], sem).start()
        return carry

    lax.fori_loop(0, rb, issue, 0)
    pltpu.make_async_copy(h_hbm.at[pl.ds(0, rb), :], buf, sem).wait()
    o_ref[...] = buf[...].astype(BF16)


def _row_block(capt, limit=512):
    return max(r for r in range(WIN_ALIGN, limit + 1, WIN_ALIGN) if capt % r == 0)


def _gather(idx_flat, h2, capt):
    rb = _row_block(capt)
    return pl.pallas_call(
        functools.partial(_gather_kernel, capt=capt, rb=rb),
        grid_spec=pltpu.PrefetchScalarGridSpec(
            num_scalar_prefetch=1,
            grid=(N_EXPERTS, capt // rb),
            in_specs=[pl.BlockSpec(memory_space=pl.ANY)],
            out_specs=pl.BlockSpec((None, rb, D_MODEL), lambda e, r, idx: (e, r, 0)),
            scratch_shapes=[pltpu.VMEM((rb, D_MODEL), F32), pltpu.SemaphoreType.DMA(())]),
        out_shape=jax.ShapeDtypeStruct((N_EXPERTS, capt, D_MODEL), BF16),
        compiler_params=_params(("arbitrary", "arbitrary")),
        name="expert_gather",
    )(idx_flat, h2)


def _ffn_kernel(xs_ref, wg_ref, wu_ref, wd_ref, y_ref, acc_ref, *, rc):
    j = pl.program_id(1)
    wg = wg_ref[...].astype(BF16)
    wu = wu_ref[...].astype(BF16)
    wd = wd_ref[...].astype(BF16)
    capt = xs_ref.shape[0]
    for r in range(capt // rc):
        sl = slice(r * rc, (r + 1) * rc)
        xs = xs_ref[sl, :]
        a = _dot(xs, wg)
        u = _dot(xs, wu)
        hmid = (a * _sigmoid(a) * u).astype(BF16)
        part = _dot(hmid, wd)

        @pl.when(j == 0)
        def _():
            acc_ref[sl, :] = part

        @pl.when(j > 0)
        def _():
            acc_ref[sl, :] += part

    @pl.when(j == pl.num_programs(1) - 1)
    def _():
        y_ref[...] = acc_ref[...].astype(BF16)


def _ffn(layer, xs, w_gate, w_up, w_down):
    capt = xs.shape[1]
    rc = _row_block(capt)
    return pl.pallas_call(
        functools.partial(_ffn_kernel, rc=rc),
        grid=(N_EXPERTS, EXPERT_FF // FF_TILE),
        in_specs=[pl.BlockSpec((None, capt, D_MODEL), lambda e, j: (e, 0, 0)),
                  pl.BlockSpec((None, None, D_MODEL, FF_TILE), lambda e, j: (layer, e, 0, j)),
                  pl.BlockSpec((None, None, D_MODEL, FF_TILE), lambda e, j: (layer, e, 0, j)),
                  pl.BlockSpec((None, None, FF_TILE, D_MODEL), lambda e, j: (layer, e, j, 0))],
        out_specs=pl.BlockSpec((None, capt, D_MODEL), lambda e, j: (e, 0, 0)),
        out_shape=jax.ShapeDtypeStruct((N_EXPERTS, capt, D_MODEL), BF16),
        scratch_shapes=[pltpu.VMEM((capt, D_MODEL), F32)],
        compiler_params=_params(("arbitrary", "arbitrary")),
        name="expert_ffn",
    )(xs, w_gate, w_up, w_down)


def _combine_kernel(start_ref, x_ref, mod_ref, slot_ref, aff_ref, fg_ref, y_hbm, o_ref, win, sem, *, final):
    i = pl.program_id(0)
    win_rows = win.shape[1]

    def copy(e, b):
        st = pl.multiple_of(start_ref[i * N_EXPERTS + e], WIN_ALIGN)
        return pltpu.make_async_copy(y_hbm.at[e, pl.ds(st, win_rows), :], win.at[b], sem.at[b])

    copy(0, 0).start()
    lane = lax.broadcasted_iota(jnp.int32, (TM, win_rows), 1)
    acc = jnp.zeros((TM, D_MODEL), F32)
    for e in range(N_EXPERTS):
        b = e % 2
        if e + 1 < N_EXPERTS:
            copy(e + 1, 1 - b).start()
        copy(e, b).wait()
        rel = slot_ref[:, e:e + 1] - start_ref[i * N_EXPERTS + e]
        onehot = jnp.where(rel == lane, aff_ref[:, e:e + 1], 0.0).astype(BF16)
        acc = acc + _dot(onehot, win[b])
    x = x_ref[...] + mod_ref[5:6, :] * acc
    if final:
        x = x * lax.rsqrt(jnp.mean(x * x, axis=-1, keepdims=True) + EPS) * fg_ref[...]
    o_ref[...] = x


def _combine(layer, starts, t, mods, slot_t, aff_t, final_g, y, nl, win_rows, final):
    n = t.shape[0]
    nt = nl if final else n // TM
    return pl.pallas_call(
        functools.partial(_combine_kernel, final=final),
        grid_spec=pltpu.PrefetchScalarGridSpec(
            num_scalar_prefetch=1,
            grid=(nt,),
            in_specs=[pl.BlockSpec((TM, D_MODEL), lambda i, s: (i, 0)),
                      pl.BlockSpec((None, None, 6, D_MODEL), lambda i, s: (layer, i // nl, 0, 0)),
                      pl.BlockSpec((TM, N_EXPERTS), lambda i, s: (i, 0)),
                      pl.BlockSpec((TM, N_EXPERTS), lambda i, s: (i, 0)),
                      pl.BlockSpec((1, D_MODEL), lambda i, s: (0, 0)),
                      pl.BlockSpec(memory_space=pl.ANY)],
            out_specs=pl.BlockSpec((TM, D_MODEL), lambda i, s: (i, 0)),
            scratch_shapes=[pltpu.VMEM((2, win_rows, D_MODEL), BF16), pltpu.SemaphoreType.DMA((2,))]),
        out_shape=jax.ShapeDtypeStruct((nt * TM, D_MODEL), F32),
        compiler_params=_params(("arbitrary",)),
        name="combine",
    )(starts, t, mods, slot_t, aff_t, final_g, y)


def _constants(lat_len, ctx_len):
    n1 = lat_len // LANES
    c64, s64 = _dft_cs(FNET_GROUP_W)
    groups = FNET_W // FNET_GROUP_W
    dftc = np.concatenate([_blockdiag(c64, groups), -_blockdiag(s64, groups)], axis=1) / math.sqrt(FNET_GROUP_W)
    c1, s1 = _dft_cs(n1)
    cs1 = np.concatenate([c1, s1], axis=0) / math.sqrt(n1)
    c2, s2 = _dft_cs(LANES)
    cs2 = np.concatenate([c2, s2], axis=1) / math.sqrt(LANES)
    cc, sc = _dft_cs(ctx_len)
    csc = np.concatenate([cc, sc], axis=1) / math.sqrt(ctx_len)
    n2 = jnp.arange(LANES, dtype=F32)[:, None]
    k1 = jnp.arange(n1, dtype=F32)[None, :]
    theta = (n2 * k1) * (2.0 * math.pi / lat_len)
    bcast = lambda a: jnp.broadcast_to(a[:, :, None], (LANES, n1, LANES))
    tri = np.triu(np.ones((LANES, LANES)))
    return dict(
        dftc=jnp.asarray(dftc, BF16), cs1=jnp.asarray(cs1, BF16), cs2=jnp.asarray(cs2, BF16),
        csc=jnp.asarray(csc, BF16), twc=bcast(jnp.cos(theta)), tws=bcast(jnp.sin(theta)),
        msq=jnp.asarray(_blockdiag(np.full((HEAD_DIM, HEAD_DIM), 1.0 / HEAD_DIM), N_Q_HEADS), BF16),
        rot=jnp.asarray(_rot_matrix(ATTN_W), BF16),
        ut=jnp.asarray(tri, BF16), lt=jnp.asarray(tri.T, BF16),
        slt_lat=jnp.asarray(np.tril(np.ones((n1, n1)), -1), BF16),
        slt_ctx=jnp.asarray(np.tril(np.ones((CTX_ROUTE_ROWS, CTX_ROUTE_ROWS)), -1), BF16),
    )


def kernel(x, c, ctx, c_ctx, w_ada, b_ada, norm1_g, w_in, q_norm_g, k_norm_g, sgu_norm_g, sgu_w, sgu_b,
           w_out, norm2_g, router_w, w_gate, w_up, w_down, final_g):
    assert x.shape[0] == 1 and ctx.shape[0] == 1
    lat_len, ctx_len = x.shape[1], ctx.shape[1]
    assert ctx_len == TM and lat_len % TK == 0 and lat_len % (LANES * FFT_BATCH) == 0
    n = lat_len + ctx_len
    nl = lat_len // TM
    n1 = lat_len // LANES
    cap_l = CAPACITY_FACTOR * lat_len // N_EXPERTS
    cap_c = CAPACITY_FACTOR * ctx_len // N_EXPERTS
    capt = cap_l + cap_c
    assert cap_c <= LANES and capt % WIN_ALIGN == 0
    win_rows = min(COMBINE_WIN, capt)
    depth = w_ada.shape[0]
    consts = _constants(lat_len, ctx_len)
    cos_t, sin_t = _rope_tables(lat_len, ctx_len)

    cvec = jnp.zeros((16, D_MODEL), F32).at[0].set(c[0]).at[1].set(c_ctx)
    mods = _mods(cvec, w_ada, b_ada)[:, :2].reshape(depth, 2, 6, D_MODEL)

    w_in_bf = w_in.astype(BF16)
    w_out_bf = w_out.astype(BF16)
    qg = jnp.tile(q_norm_g, (1, N_Q_HEADS)).reshape(depth, 1, ATTN_W)
    kg = jnp.tile(k_norm_g, (1, N_KV_HEADS)).reshape(depth, 1, KV_W)
    wcat = jnp.transpose(sgu_w, (0, 2, 1, 3)).reshape(depth, CHUNK, SGU_GROUPS * CHUNK).astype(BF16)
    sbias = jnp.repeat(jnp.transpose(sgu_b, (0, 2, 1)), SGU_GROUP_W, axis=2)
    rwt = jnp.transpose(router_w, (0, 2, 1))
    g1 = norm1_g.reshape(depth, 1, D_MODEL)
    g2 = norm2_g.reshape(depth, 1, D_MODEL)
    sgug = sgu_norm_g.reshape(depth, 1, SGU_W)
    fg = final_g.reshape(1, D_MODEL)

    t = jnp.concatenate([x[0], ctx[0]], axis=0)
    for layer in range(depth):
        final = layer == depth - 1
        qt, k, vt, sgu, g = _proj(layer, t, mods, g1, w_in_bf, cos_t, sin_t, qg, kg, consts['msq'], consts['rot'],
                                  sgug, wcat, sbias, consts['dftc'], nl)
        o = _attention(qt, k, vt, lat_len, ctx_len)
        y_lat, y_ctx = _fourier_positions(g, lat_len, ctx_len, consts)
        t, h2, aff = _outproj(layer, t, mods, o, sgu, y_lat, y_ctx, w_out_bf, g2, rwt, nl)

        aff_lat = aff[:, :lat_len].reshape(N_EXPERTS * n1, LANES)
        aff_ctx = jnp.pad(aff[:, lat_len:].reshape(N_EXPERTS, ctx_len // LANES, LANES),
                          ((0, 0), (0, CTX_ROUTE_ROWS - ctx_len // LANES), (0, 0))
                          ).reshape(N_EXPERTS * CTX_ROUTE_ROWS, LANES)
        slot_l, off_l, idx_l = _route(aff_lat, n1, cap_l, consts['ut'], consts['lt'], consts['slt_lat'])
        slot_c, _, idx_c = _route(aff_ctx, CTX_ROUTE_ROWS, cap_c, consts['ut'], consts['lt'], consts['slt_ctx'])

        idx = jnp.concatenate([idx_l[:, :cap_l], idx_c[:, :cap_c] + lat_len], axis=1)
        slot_c = slot_c.reshape(N_EXPERTS, CTX_ROUTE_ROWS * LANES)[:, :ctx_len]
        slot = jnp.concatenate([slot_l.reshape(N_EXPERTS, lat_len),
                                jnp.where(slot_c >= 0, slot_c + cap_l, -1)], axis=1)
        lo = off_l.reshape(N_EXPERTS, n1, LANES)[:, ::TM // LANES, 0]
        starts = jnp.minimum((lo // WIN_ALIGN) * WIN_ALIGN, capt - win_rows)
        starts = jnp.concatenate([starts, jnp.full((N_EXPERTS, 1), capt - win_rows, jnp.int32)], axis=1)

        xs = _gather(idx.reshape(-1), h2, capt)
        y = _ffn(layer, xs, w_gate, w_up, w_down)
        t = _combine(layer, starts.T.reshape(-1), t, mods, slot.T, aff.T, fg, y, nl, win_rows, final)
    return t.reshape(1, lat_len, D_MODEL)
```

```python
import functools
import math

import numpy as np
import jax
import jax.numpy as jnp
from jax import lax
from jax.experimental import pallas as pl
from jax.experimental.pallas import tpu as pltpu

D_MODEL = 1024
DEPTH = 4
GRID_W = 64
HEAD_DIM = 64
ATTN_W = 512
N_Q_HEADS = 8
N_KV_HEADS = 2
Q_PER_KV = 4
KV_W = 128
AXIS_DIM = 32
ROPE_BASE = 10000.0
SGU_W = 256
SGU_GROUP_W = 64
SGU_GROUPS = 4
CHUNK = 128
FNET_W = 256
FNET_GROUP_W = 64
IN_W = 1536
N_EXPERTS = 16
CAPACITY_FACTOR = 2
EXPERT_FF = 2048
EPS = 1e-6

LANES = 128
TM = 256
TK = 2048
FF_TILE = 512
COMBINE_BLK = 128
COMBINE_ALIGN = 64
COMBINE_MAX_BLKS = -(-(COMBINE_ALIGN - 1 + TM) // COMBINE_BLK)
COMBINE_EXPERTS = 8
BF16_ROWS = 16
GATHER_UNROLL = 8
CTX_ROUTE_ROWS = 16
V_ROWS = HEAD_DIM + 16
LOG2E = math.log2(math.e)
VMEM_LIMIT = 56 * 1024 * 1024

F32 = jnp.float32
BF16 = jnp.bfloat16


def _dot(a, b):
    return jnp.dot(a, b, preferred_element_type=F32)


def _dot_nt(a, b):
    return lax.dot_general(a, b, (((1,), (1,)), ((), ())), preferred_element_type=F32)


def _split(a):
    hi = a.astype(BF16)
    lo = (a - hi.astype(F32)).astype(BF16)
    return hi, lo


def _dot_x2(a, b_bf16):
    hi, lo = _split(a)
    return _dot(hi, b_bf16) + _dot(lo, b_bf16)


def _dot_x3(a, b):
    ah, al = _split(a)
    bh, bl = _split(b)
    return _dot(ah, bh) + (_dot(al, bh) + _dot(ah, bl))


def _dot_nt_x3(a, b):
    ah, al = _split(a)
    bh, bl = _split(b)
    return _dot_nt(ah, bh) + (_dot_nt(al, bh) + _dot_nt(ah, bl))


def _sigmoid(a):
    return 1.0 / (1.0 + jnp.exp(-a))


def _params(sem, vmem=VMEM_LIMIT):
    return pltpu.CompilerParams(dimension_semantics=sem, vmem_limit_bytes=vmem)


def _blockdiag(block, n):
    k = block.shape[0]
    out = np.zeros((k * n, block.shape[1] * n), np.float64)
    for i in range(n):
        out[i * k:(i + 1) * k, i * block.shape[1]:(i + 1) * block.shape[1]] = block
    return out


def _dft_cs(n):
    a = 2.0 * np.pi * np.outer(np.arange(n), np.arange(n)) / n
    return np.cos(a), np.sin(a)


def _rot_matrix(width):
    h = AXIS_DIM // 2
    r = np.zeros((width, width), np.float64)
    for j in range(width):
        if (j % AXIS_DIM) < h:
            r[j + h, j] = -1.0
        else:
            r[j - h, j] = 1.0
    return r


def _rope_tables(L, C):
    rows = L // GRID_W
    r, col = jnp.meshgrid(jnp.arange(rows, dtype=F32), jnp.arange(GRID_W, dtype=F32), indexing='ij')
    inv = 1.0 / (ROPE_BASE ** (jnp.arange(0, AXIS_DIM, 2, dtype=F32) / AXIS_DIM))
    ar = r.reshape(-1, 1) * inv
    ac = col.reshape(-1, 1) * inv
    cos = jnp.concatenate([jnp.cos(ar), jnp.cos(ar), jnp.cos(ac), jnp.cos(ac)], axis=-1)
    sin = jnp.concatenate([jnp.sin(ar), jnp.sin(ar), jnp.sin(ac), jnp.sin(ac)], axis=-1)
    cos = jnp.concatenate([cos, jnp.ones((C, HEAD_DIM), F32)], axis=0)
    sin = jnp.concatenate([sin, jnp.zeros((C, HEAD_DIM), F32)], axis=0)
    return jnp.tile(cos, (1, 2)), jnp.tile(sin, (1, 2))


def _mods_kernel(cv_ref, w_ref, b_ref, o_ref):
    cv = cv_ref[...]
    s = cv * _sigmoid(cv)
    o_ref[...] = _dot_x3(s, w_ref[...]) + b_ref[...]


def _mods(cvec, w_ada, b_ada):
    depth, d, six_d = w_ada.shape
    rows = cvec.shape[0]
    tn = 1024
    return pl.pallas_call(
        _mods_kernel,
        grid=(depth, six_d // tn),
        in_specs=[pl.BlockSpec((rows, d), lambda l, j: (0, 0)),
                  pl.BlockSpec((None, d, tn), lambda l, j: (l, 0, j)),
                  pl.BlockSpec((None, 1, tn), lambda l, j: (l, 0, j))],
        out_specs=pl.BlockSpec((None, rows, tn), lambda l, j: (l, 0, j)),
        out_shape=jax.ShapeDtypeStruct((depth, rows, six_d), F32),
        compiler_params=_params(("arbitrary", "arbitrary")),
        name="adaln_mods",
    )(cvec, w_ada, b_ada.reshape(depth, 1, six_d))


def _head_norm_rope(t, gain, cos, sin, msq, rot):
    ms = _dot_x2(t * t, msq)
    tn = t * lax.rsqrt(ms + EPS) * gain
    return tn * cos + _dot(tn.astype(BF16), rot) * sin


def _gelu_tanh(z):
    return 0.5 * z * (1.0 + jnp.tanh(math.sqrt(2.0 / math.pi) * (z + 0.044715 * (z * z * z))))


def _proj_kernel(x_ref, mod_ref, g1_ref, win_ref, cos_ref, sin_ref, qg_ref, kg_ref, msq_ref, rot_ref,
                 sgug_ref, wcat_ref, sbias_ref, dft_ref,
                 qt_ref, k_ref, vt_ref, sgu_ref, g_ref):
    x = x_ref[...]
    shift = mod_ref[0:1, :]
    scale = mod_ref[1:2, :]
    h = x * lax.rsqrt(jnp.mean(x * x, axis=-1, keepdims=True) + EPS) * g1_ref[...]
    h = h * (1.0 + scale) + shift
    p = _dot(h.astype(BF16), win_ref[...])

    cos2 = cos_ref[...]
    sin2 = sin_ref[...]
    cos8 = jnp.concatenate([cos2] * 4, axis=1)
    sin8 = jnp.concatenate([sin2] * 4, axis=1)
    qo = _head_norm_rope(p[:, :ATTN_W], qg_ref[...], cos8, sin8, msq_ref[...], rot_ref[...])
    qot = (qo * (HEAD_DIM ** -0.5 * LOG2E)).T
    for hh in range(N_Q_HEADS):
        qt_ref[hh] = qot[hh * HEAD_DIM:(hh + 1) * HEAD_DIM, :].astype(BF16)

    ko = _head_norm_rope(p[:, ATTN_W:ATTN_W + KV_W], kg_ref[...], cos2, sin2,
                         msq_ref[:KV_W, :KV_W], rot_ref[:KV_W, :KV_W])
    vt = p[:, ATTN_W + KV_W:ATTN_W + 2 * KV_W].T
    ones_rows = jnp.where(lax.broadcasted_iota(jnp.int32, (V_ROWS - HEAD_DIM, LANES), 0) == 0, 1.0, 0.0)
    for hh in range(N_KV_HEADS):
        k_ref[hh] = ko[:, hh * HEAD_DIM:(hh + 1) * HEAD_DIM].astype(BF16)
        for cc in range(TM // LANES):
            vt_ref[hh, cc, :HEAD_DIM, :] = vt[hh * HEAD_DIM:(hh + 1) * HEAD_DIM,
                                              cc * LANES:(cc + 1) * LANES].astype(BF16)
            vt_ref[hh, cc, HEAD_DIM:, :] = ones_rows.astype(BF16)

    o3 = ATTN_W + 2 * KV_W
    ge = _gelu_tanh(p[:, o3:o3 + 2 * SGU_W])
    u = ge[:, :SGU_W]
    vv = ge[:, SGU_W:]
    vms = _dot_x2(vv * vv, msq_ref[:SGU_W, :SGU_W])
    vn = vv * lax.rsqrt(vms + EPS) * sgug_ref[...]
    lane_group = lax.broadcasted_iota(jnp.int32, (CHUNK, SGU_W), 1) // SGU_GROUP_W
    for cc in range(TM // CHUNK):
        vc = vn[cc * CHUNK:(cc + 1) * CHUNK]
        vstack = jnp.concatenate([jnp.where(lane_group == gg, vc, 0.0) for gg in range(SGU_GROUPS)],
                                 axis=0).astype(BF16)
        s = _dot(wcat_ref[...], vstack) + sbias_ref[...]
        sgu_ref[cc * CHUNK:(cc + 1) * CHUNK, :] = (u[cc * CHUNK:(cc + 1) * CHUNK] * s).astype(BF16)

    o4 = o3 + 2 * SGU_W
    g_ref[...] = _dot(p[:, o4:].astype(BF16), dft_ref[...]).astype(BF16)


def _proj(layer, t, mods, norm1_g, w_in_bf, cos_t, sin_t, qg, kg, msq, rot, sgug, wcat, sbias, dftc, nl):
    n = t.shape[0]
    nt = n // TM
    const = lambda *shape: pl.BlockSpec(shape, lambda i: (0,) * len(shape))
    lay = lambda *shape: pl.BlockSpec((None,) + shape, lambda i: (layer,) + (0,) * len(shape))
    return pl.pallas_call(
        _proj_kernel,
        grid=(nt,),
        in_specs=[pl.BlockSpec((TM, D_MODEL), lambda i: (i, 0)),
                  pl.BlockSpec((None, None, 6, D_MODEL), lambda i: (layer, i // nl, 0, 0)),
                  lay(1, D_MODEL),
                  lay(D_MODEL, IN_W),
                  pl.BlockSpec((TM, 2 * HEAD_DIM), lambda i: (i, 0)),
                  pl.BlockSpec((TM, 2 * HEAD_DIM), lambda i: (i, 0)),
                  lay(1, ATTN_W), lay(1, KV_W),
                  const(ATTN_W, ATTN_W), const(ATTN_W, ATTN_W),
                  lay(1, SGU_W), lay(CHUNK, SGU_GROUPS * CHUNK), lay(CHUNK, SGU_W),
                  const(FNET_W, 2 * FNET_W)],
        out_specs=[pl.BlockSpec((N_Q_HEADS, HEAD_DIM, TM), lambda i: (0, 0, i)),
                   pl.BlockSpec((N_KV_HEADS, TM, HEAD_DIM), lambda i: (0, i, 0)),
                   pl.BlockSpec((N_KV_HEADS, TM // LANES, V_ROWS, LANES), lambda i: (0, i, 0, 0)),
                   pl.BlockSpec((TM, SGU_W), lambda i: (i, 0)),
                   pl.BlockSpec((TM, 2 * FNET_W), lambda i: (i, 0))],
        out_shape=[jax.ShapeDtypeStruct((N_Q_HEADS, HEAD_DIM, n), BF16),
                   jax.ShapeDtypeStruct((N_KV_HEADS, n, HEAD_DIM), BF16),
                   jax.ShapeDtypeStruct((N_KV_HEADS, n // LANES, V_ROWS, LANES), BF16),
                   jax.ShapeDtypeStruct((n, SGU_W), BF16),
                   jax.ShapeDtypeStruct((n, 2 * FNET_W), BF16)],
        compiler_params=_params(("arbitrary",)),
        name="proj",
    )(t, mods, norm1_g, w_in_bf, cos_t, sin_t, qg, kg, msq, rot, sgug, wcat, sbias, dftc)


def _attn_kernel(qt_ref, k_ref, vt_ref, o_ref, *st_sc, n_lat_tiles, n_lat_chunks, ctx_len, lat_len):
    i = pl.program_id(1)

    def step(k, vt, carry):
        out = []
        nk = k.shape[0]
        for g in range(Q_PER_KV):
            st_sc[g][:nk, :] = _dot(k, qt_ref[g])
        for g in range(Q_PER_KV):
            m_old, acc = carry[g]
            st = st_sc[g][:nk, :]
            m_new = jnp.maximum(m_old, jnp.max(st, axis=0, keepdims=True))
            alpha = jnp.exp2(m_old - m_new)
            pt = jnp.exp2(st - m_new).astype(BF16)
            out.append((m_new, alpha * acc + _dot(vt, pt)))
        return tuple(out)

    sub = TK // LANES

    def body(j, carry):
        k = k_ref[pl.ds(pl.multiple_of(j * TK, TK), TK), :]
        vt = jnp.concatenate([vt_ref[j * sub + c] for c in range(sub)], axis=1)
        return step(k, vt, carry)

    init = tuple((jnp.full((1, TM), -jnp.inf, F32), jnp.zeros((V_ROWS, TM), F32)) for _ in range(Q_PER_KV))
    carry = lax.fori_loop(0, jnp.where(i < n_lat_tiles, n_lat_chunks, 0), body, init)
    cb = lat_len // LANES
    vt_c = jnp.concatenate([vt_ref[cb + c] for c in range(ctx_len // LANES)], axis=1)
    carry = step(k_ref[lat_len:lat_len + ctx_len, :], vt_c, carry)

    o2 = jnp.concatenate([acc[:HEAD_DIM, :] / acc[HEAD_DIM:HEAD_DIM + 1, :] for _, acc in carry],
                         axis=0)
    o_ref[...] = o2.T.astype(BF16)


def _attention(qt, k, vt, lat_len, ctx_len):
    n = k.shape[1]
    nt = n // TM
    kern = functools.partial(_attn_kernel, n_lat_tiles=lat_len // TM, n_lat_chunks=lat_len // TK,
                             ctx_len=ctx_len, lat_len=lat_len)
    return pl.pallas_call(
        kern,
        grid=(N_KV_HEADS, nt),
        in_specs=[pl.BlockSpec((Q_PER_KV, HEAD_DIM, TM), lambda h, i: (h, 0, i)),
                  pl.BlockSpec((None, n, HEAD_DIM), lambda h, i: (h, 0, 0)),
                  pl.BlockSpec((None, n // LANES, V_ROWS, LANES), lambda h, i: (h, 0, 0, 0))],
        out_specs=pl.BlockSpec((TM, Q_PER_KV * HEAD_DIM), lambda h, i: (i, h)),
        out_shape=jax.ShapeDtypeStruct((n, ATTN_W), BF16),
        scratch_shapes=[pltpu.VMEM((TK, TM), F32) for _ in range(Q_PER_KV)],
        compiler_params=_params(("arbitrary", "arbitrary")),
        name="attention",
    )(qt, k, vt)


FFT_BATCH = 4


def _fft1_kernel(g_ref, cs_ref, twc_ref, tws_ref, o_ref):
    n1 = g_ref.shape[0]
    w = 2 * FNET_W
    for b in range(FFT_BATCH):
        xg = g_ref[:, b * w:(b + 1) * w]
        pq = _dot(cs_ref[...], xg)
        ar = pq[:n1, :FNET_W] + pq[n1:, FNET_W:]
        ai = pq[:n1, FNET_W:] - pq[n1:, :FNET_W]
        c = jnp.concatenate([twc_ref[b]] * (FNET_W // LANES), axis=1)
        s = jnp.concatenate([tws_ref[b]] * (FNET_W // LANES), axis=1)
        o_ref[:, b * w:b * w + FNET_W] = (ar * c + ai * s).astype(BF16)
        o_ref[:, b * w + FNET_W:(b + 1) * w] = (ai * c - ar * s).astype(BF16)


def _fft2_kernel(t_ref, cs_ref, o_ref):
    for b in range(FFT_BATCH):
        tb = t_ref[b * LANES:(b + 1) * LANES, :]
        st = jnp.concatenate([tb[:, :FNET_W], tb[:, FNET_W:]], axis=0)
        o_ref[:, b * FNET_W:(b + 1) * FNET_W] = _dot(cs_ref[...], st).astype(BF16)


def _fftc_kernel(g_ref, cs_ref, o_ref):
    gb = g_ref[...]
    st = jnp.concatenate([gb[:, :FNET_W], gb[:, FNET_W:]], axis=0)
    o_ref[...] = _dot(cs_ref[...], st).astype(BF16)


def _fourier_positions(g, lat_len, ctx_len, consts):
    n1 = lat_len // LANES
    w = 2 * FNET_W
    g_lat = g[:lat_len].reshape(n1, LANES * w)
    t = pl.pallas_call(
        _fft1_kernel,
        grid=(LANES // FFT_BATCH,),
        in_specs=[pl.BlockSpec((n1, FFT_BATCH * w), lambda j: (0, j)),
                  pl.BlockSpec((2 * n1, n1), lambda j: (0, 0)),
                  pl.BlockSpec((FFT_BATCH, n1, LANES), lambda j: (j, 0, 0)),
                  pl.BlockSpec((FFT_BATCH, n1, LANES), lambda j: (j, 0, 0))],
        out_specs=pl.BlockSpec((n1, FFT_BATCH * w), lambda j: (0, j)),
        out_shape=jax.ShapeDtypeStruct((n1, LANES * w), BF16),
        compiler_params=_params(("arbitrary",)),
        name="fft_stage1",
    )(g_lat, consts['cs1'], consts['twc'], consts['tws'])
    y_lat = pl.pallas_call(
        _fft2_kernel,
        grid=(n1 // FFT_BATCH,),
        in_specs=[pl.BlockSpec((FFT_BATCH * LANES, w), lambda j: (j, 0)),
                  pl.BlockSpec((LANES, 2 * LANES), lambda j: (0, 0))],
        out_specs=pl.BlockSpec((LANES, FFT_BATCH * FNET_W), lambda j: (0, j)),
        out_shape=jax.ShapeDtypeStruct((LANES, n1 * FNET_W), BF16),
        compiler_params=_params(("arbitrary",)),
        name="fft_stage2",
    )(t.reshape(n1 * LANES, w), consts['cs2'])
    y_ctx = pl.pallas_call(
        _fftc_kernel,
        grid=(1,),
        in_specs=[pl.BlockSpec((ctx_len, w), lambda j: (lat_len // ctx_len, 0)),
                  pl.BlockSpec((ctx_len, 2 * ctx_len), lambda j: (0, 0))],
        out_specs=pl.BlockSpec((ctx_len, FNET_W), lambda j: (0, 0)),
        out_shape=jax.ShapeDtypeStruct((ctx_len, FNET_W), BF16),
        compiler_params=_params(("arbitrary",)),
        name="fft_ctx",
    )(g, consts['csc'])
    return y_lat.reshape(lat_len, FNET_W), y_ctx


def _outproj_kernel(x_ref, mod_ref, o_ref, sgu_ref, yl_ref, yc_ref, wout_ref, g2_ref, rwt_ref,
                    xo_ref, h2_ref, aff_ref, *, n_lat_tiles):
    i = pl.program_id(0)
    yf = jnp.where(i < n_lat_tiles, yl_ref[...], yc_ref[...])
    mix = _dot(o_ref[...], wout_ref[:ATTN_W, :])
    mix = mix + _dot(sgu_ref[...], wout_ref[ATTN_W:ATTN_W + SGU_W, :])
    mix = mix + _dot(yf, wout_ref[ATTN_W + SGU_W:, :])
    x = x_ref[...] + mod_ref[2:3, :] * mix
    xo_ref[...] = x
    h = x * lax.rsqrt(jnp.mean(x * x, axis=-1, keepdims=True) + EPS) * g2_ref[...]
    h = h * (1.0 + mod_ref[4:5, :]) + mod_ref[3:4, :]
    h2_ref[...] = h
    logits = _dot_nt_x3(rwt_ref[...], h)
    z = jnp.exp(logits - jnp.max(logits, axis=0, keepdims=True))
    aff_ref[...] = z / jnp.sum(z, axis=0, keepdims=True)


def _outproj(layer, t, mods, o, sgu, y_lat, y_ctx, w_out_bf, norm2_g, rwt, nl):
    n = t.shape[0]
    nt = n // TM
    lay = lambda *shape: pl.BlockSpec((None,) + shape, lambda i: (layer,) + (0,) * len(shape))
    return pl.pallas_call(
        functools.partial(_outproj_kernel, n_lat_tiles=nl),
        grid=(nt,),
        in_specs=[pl.BlockSpec((TM, D_MODEL), lambda i: (i, 0)),
                  pl.BlockSpec((None, None, 6, D_MODEL), lambda i: (layer, i // nl, 0, 0)),
                  pl.BlockSpec((TM, ATTN_W), lambda i: (i, 0)),
                  pl.BlockSpec((TM, SGU_W), lambda i: (i, 0)),
                  pl.BlockSpec((TM, FNET_W), lambda i: (jnp.minimum(i, nl - 1), 0)),
                  pl.BlockSpec((TM, FNET_W), lambda i: (0, 0)),
                  lay(D_MODEL, D_MODEL), lay(1, D_MODEL), lay(N_EXPERTS, D_MODEL)],
        out_specs=[pl.BlockSpec((TM, D_MODEL), lambda i: (i, 0)),
                   pl.BlockSpec((TM, D_MODEL), lambda i: (i, 0)),
                   pl.BlockSpec((N_EXPERTS, TM), lambda i: (0, i))],
        out_shape=[jax.ShapeDtypeStruct((n, D_MODEL), F32),
                   jax.ShapeDtypeStruct((n, D_MODEL), F32),
                   jax.ShapeDtypeStruct((N_EXPERTS, n), F32)],
        compiler_params=_params(("arbitrary",)),
        name="outproj_router",
    )(t, mods, o, sgu, y_lat, y_ctx, w_out_bf, norm2_g, rwt)


def _route_kernel(a_ref, ut_ref, lt_ref, slt_ref, slot_ref, off_ref, idx_ref, *, rows, cap, capw):
    e_n = N_EXPERTS
    a3 = a_ref[...].reshape(e_n, rows, LANES)

    def count(mask):
        c = jnp.sum(jnp.where(mask, 1.0, 0.0), axis=1, keepdims=True)
        return jnp.sum(c, axis=2, keepdims=True)

    def as_float(bits):
        return lax.bitcast_convert_type(bits, F32)

    def search(it, thr):
        cand = thr | jnp.left_shift(jnp.int32(1), 30 - it)
        return jnp.where(count(a3 >= as_float(cand)) >= cap, cand, thr)

    thr = lax.fori_loop(0, 31, search, jnp.zeros((e_n, 1, 1), jnp.int32))
    above = as_float(thr + 1)
    gt = a3 >= above
    eq = jnp.logical_and(a3 >= as_float(thr), a3 < above)
    need = cap - count(gt)

    def ranks(mask3):
        m = jnp.where(mask3, 1.0, 0.0).reshape(e_n * rows, LANES)
        cin = _dot(m.astype(BF16), ut_ref[...])
        tot = cin[:, LANES - 1:LANES]
        totb = jnp.broadcast_to(tot, (e_n * rows, LANES)).astype(BF16)
        offs = [_dot(slt_ref[...], totb[e * rows:(e + 1) * rows]) for e in range(e_n)]
        off = jnp.concatenate(offs, axis=0)
        return m, cin, off

    m_eq, cin_eq, off_eq = ranks(eq)
    eq_rank = (off_eq + cin_eq - m_eq).reshape(e_n, rows, LANES)
    sel = gt | (eq & (eq_rank < need))
    m_sel, cin_sel, off_sel = ranks(sel)
    pos = off_sel + cin_sel - m_sel
    slot_ref[...] = jnp.where(m_sel > 0.5, pos, -1.0).astype(jnp.int32)
    off_ref[...] = off_sel.astype(jnp.int32)

    r_lane = lax.broadcasted_iota(jnp.int32, (1, capw), 1).astype(F32)
    row_iota = lax.broadcasted_iota(jnp.int32, (rows, capw), 0).astype(F32)
    for e in range(e_n):
        sl = slice(e * rows, (e + 1) * rows)
        off_e = off_sel[sl, 0:1]
        end_e = off_e + cin_sel[sl, LANES - 1:LANES]
        a_of_r = jnp.sum(jnp.where(end_e <= r_lane, 1.0, 0.0), axis=0, keepdims=True)
        onehot = row_iota == a_of_r
        off_at = jnp.sum(jnp.where(onehot, off_e, 0.0), axis=0, keepdims=True)
        cin_t = _dot_nt(lt_ref[...], m_sel[sl].astype(BF16))
        crow = _dot(cin_t.astype(BF16), jnp.where(onehot, 1.0, 0.0).astype(BF16))
        b_of_r = jnp.sum(jnp.where(crow <= r_lane - off_at, 1.0, 0.0), axis=0, keepdims=True)
        idx_ref[e:e + 1, :] = (a_of_r * LANES + b_of_r).astype(jnp.int32)


def _route(aff2, rows, cap, ut, lt, slt):
    capw = max(cap, LANES)
    full = lambda *shape: pl.BlockSpec(shape, lambda i: (0,) * len(shape))
    return pl.pallas_call(
        functools.partial(_route_kernel, rows=rows, cap=cap, capw=capw),
        grid=(1,),
        in_specs=[full(N_EXPERTS * rows, LANES), full(LANES, LANES), full(LANES, LANES), full(rows, rows)],
        out_specs=[full(N_EXPERTS * rows, LANES), full(N_EXPERTS * rows, LANES), full(N_EXPERTS, capw)],
        out_shape=[jax.ShapeDtypeStruct((N_EXPERTS * rows, LANES), jnp.int32),
                   jax.ShapeDtypeStruct((N_EXPERTS * rows, LANES), jnp.int32),
                   jax.ShapeDtypeStruct((N_EXPERTS, capw), jnp.int32)],
        compiler_params=_params(("arbitrary",)),
        name="route",
    )(aff2, ut, lt, slt)


def _gather_kernel(idx_ref, h_hbm, o_ref, buf, sem, *, capt, rb):
    base = pl.program_id(0) * capt + pl.program_id(1) * rb

    def issue(r, carry):
        tok = idx_ref[base + r]
        pltpu.make_async_copy(h_hbm.at[pl.ds(tok, 1), :], buf.at[pl.ds(r, 1), :], sem).start()
        return carry

    lax.fori_loop(0, rb, issue, 0, unroll=GATHER_UNROLL)
    pltpu.make_async_copy(h_hbm.at[pl.ds(0, rb), :], buf, sem).wait()
    o_ref[...] = buf[...].astype(BF16)


def _row_block(capt, limit=512):
    return max(r for r in range(BF16_ROWS, limit + 1, BF16_ROWS) if capt % r == 0)


def _gather(idx_flat, h2, capt):
    rb = _row_block(capt)
    return pl.pallas_call(
        functools.partial(_gather_kernel, capt=capt, rb=rb),
        grid_spec=pltpu.PrefetchScalarGridSpec(
            num_scalar_prefetch=1,
            grid=(N_EXPERTS, capt // rb),
            in_specs=[pl.BlockSpec(memory_space=pl.ANY)],
            out_specs=pl.BlockSpec((None, rb, D_MODEL), lambda e, r, idx: (e, r, 0)),
            scratch_shapes=[pltpu.VMEM((rb, D_MODEL), F32), pltpu.SemaphoreType.DMA(())]),
        out_shape=jax.ShapeDtypeStruct((N_EXPERTS, capt, D_MODEL), BF16),
        compiler_params=_params(("arbitrary", "arbitrary")),
        name="expert_gather",
    )(idx_flat, h2)


def _ffn_kernel(xs_ref, wg_ref, wu_ref, wd_ref, y_ref, acc_ref, *, rc):
    j = pl.program_id(1)
    wg = wg_ref[...].astype(BF16)
    wu = wu_ref[...].astype(BF16)
    wd = wd_ref[...].astype(BF16)
    capt = xs_ref.shape[0]
    for r in range(capt // rc):
        sl = slice(r * rc, (r + 1) * rc)
        xs = xs_ref[sl, :]
        a = _dot(xs, wg)
        u = _dot(xs, wu)
        hmid = (a * _sigmoid(a) * u).astype(BF16)
        part = _dot(hmid, wd)

        @pl.when(j == 0)
        def _():
            acc_ref[sl, :] = part

        @pl.when(j > 0)
        def _():
            acc_ref[sl, :] += part

    @pl.when(j == pl.num_programs(1) - 1)
    def _():
        y_ref[...] = acc_ref[...].astype(BF16)


def _ffn(layer, xs, w_gate, w_up, w_down):
    capt = xs.shape[1]
    rc = _row_block(capt)
    return pl.pallas_call(
        functools.partial(_ffn_kernel, rc=rc),
        grid=(N_EXPERTS, EXPERT_FF // FF_TILE),
        in_specs=[pl.BlockSpec((None, capt, D_MODEL), lambda e, j: (e, 0, 0)),
                  pl.BlockSpec((None, None, D_MODEL, FF_TILE), lambda e, j: (layer, e, 0, j)),
                  pl.BlockSpec((None, None, D_MODEL, FF_TILE), lambda e, j: (layer, e, 0, j)),
                  pl.BlockSpec((None, None, FF_TILE, D_MODEL), lambda e, j: (layer, e, j, 0))],
        out_specs=pl.BlockSpec((None, capt, D_MODEL), lambda e, j: (e, 0, 0)),
        out_shape=jax.ShapeDtypeStruct((N_EXPERTS, capt, D_MODEL), BF16),
        scratch_shapes=[pltpu.VMEM((capt, D_MODEL), F32)],
        compiler_params=_params(("arbitrary", "arbitrary")),
        name="expert_ffn",
    )(xs, w_gate, w_up, w_down)


def _combine_kernel(start_ref, extra_ref, x_ref, mod_ref, slot_ref, aff_ref, fg_ref, y_hbm, o_ref,
                    y_sc, acc_sc, sem, *, e0, ne, capt, final):
    i = pl.program_id(0)

    @pl.when(i == 0)
    def _():
        y_sc[:, capt:, :] = jnp.zeros((ne, y_sc.shape[1] - capt, D_MODEL), BF16)
        cp = pltpu.make_async_copy(y_hbm.at[pl.ds(e0, ne)], y_sc.at[:, pl.ds(0, capt), :], sem)
        cp.start()
        cp.wait()

    lane = lax.broadcasted_iota(jnp.int32, (TM, COMBINE_BLK), 1)

    def block(e, row0):
        rel = slot_ref[:, e0 + e:e0 + e + 1] - row0
        onehot = jnp.where(rel == lane, aff_ref[:, e0 + e:e0 + e + 1], 0.0).astype(BF16)
        return _dot(onehot, y_sc[e, pl.ds(row0, COMBINE_BLK), :])

    def start(e):
        return pl.multiple_of(start_ref[i * N_EXPERTS + e0 + e], COMBINE_ALIGN)

    acc = block(0, start(0))
    for e in range(1, ne):
        acc = acc + block(e, start(e))
    acc_sc[...] = acc
    for e in range(ne):
        for kb in range(1, COMBINE_MAX_BLKS):
            @pl.when(extra_ref[i * N_EXPERTS + e0 + e] >= kb)
            def _():
                acc_sc[...] += block(e, start(e) + kb * COMBINE_BLK)

    x = x_ref[...] + mod_ref[5:6, :] * acc_sc[...]
    if final:
        x = x * lax.rsqrt(jnp.mean(x * x, axis=-1, keepdims=True) + EPS) * fg_ref[...]
    o_ref[...] = x


def _combine(layer, starts, extras, t, mods, slot_t, aff_t, final_g, y, nl, e0, ne, nt, final):
    capt = y.shape[1]
    ypad = -(-capt // COMBINE_BLK) * COMBINE_BLK
    return pl.pallas_call(
        functools.partial(_combine_kernel, e0=e0, ne=ne, capt=capt, final=final),
        grid_spec=pltpu.PrefetchScalarGridSpec(
            num_scalar_prefetch=2,
            grid=(nt,),
            in_specs=[pl.BlockSpec((TM, D_MODEL), lambda i, s, x: (i, 0)),
                      pl.BlockSpec((None, None, 6, D_MODEL), lambda i, s, x: (layer, i // nl, 0, 0)),
                      pl.BlockSpec((TM, N_EXPERTS), lambda i, s, x: (i, 0)),
                      pl.BlockSpec((TM, N_EXPERTS), lambda i, s, x: (i, 0)),
                      pl.BlockSpec((1, D_MODEL), lambda i, s, x: (0, 0)),
                      pl.BlockSpec(memory_space=pl.ANY)],
            out_specs=pl.BlockSpec((TM, D_MODEL), lambda i, s, x: (i, 0)),
            scratch_shapes=[pltpu.VMEM((ne, ypad, D_MODEL), BF16), pltpu.VMEM((TM, D_MODEL), F32),
                            pltpu.SemaphoreType.DMA(())]),
        out_shape=jax.ShapeDtypeStruct((nt * TM, D_MODEL), F32),
        compiler_params=_params(("arbitrary",)),
        name="combine",
    )(starts, extras, t, mods, slot_t, aff_t, final_g, y)


def _constants(lat_len, ctx_len):
    n1 = lat_len // LANES
    c64, s64 = _dft_cs(FNET_GROUP_W)
    groups = FNET_W // FNET_GROUP_W
    dftc = np.concatenate([_blockdiag(c64, groups), -_blockdiag(s64, groups)], axis=1) / math.sqrt(FNET_GROUP_W)
    c1, s1 = _dft_cs(n1)
    cs1 = np.concatenate([c1, s1], axis=0) / math.sqrt(n1)
    c2, s2 = _dft_cs(LANES)
    cs2 = np.concatenate([c2, s2], axis=1) / math.sqrt(LANES)
    cc, sc = _dft_cs(ctx_len)
    csc = np.concatenate([cc, sc], axis=1) / math.sqrt(ctx_len)
    n2 = jnp.arange(LANES, dtype=F32)[:, None]
    k1 = jnp.arange(n1, dtype=F32)[None, :]
    theta = (n2 * k1) * (2.0 * math.pi / lat_len)
    bcast = lambda a: jnp.broadcast_to(a[:, :, None], (LANES, n1, LANES))
    tri = np.triu(np.ones((LANES, LANES)))
    return dict(
        dftc=jnp.asarray(dftc, BF16), cs1=jnp.asarray(cs1, BF16), cs2=jnp.asarray(cs2, BF16),
        csc=jnp.asarray(csc, BF16), twc=bcast(jnp.cos(theta)), tws=bcast(jnp.sin(theta)),
        msq=jnp.asarray(_blockdiag(np.full((HEAD_DIM, HEAD_DIM), 1.0 / HEAD_DIM), N_Q_HEADS), BF16),
        rot=jnp.asarray(_rot_matrix(ATTN_W), BF16),
        ut=jnp.asarray(tri, BF16), lt=jnp.asarray(tri.T, BF16),
        slt_lat=jnp.asarray(np.tril(np.ones((n1, n1)), -1), BF16),
        slt_ctx=jnp.asarray(np.tril(np.ones((CTX_ROUTE_ROWS, CTX_ROUTE_ROWS)), -1), BF16),
    )


def kernel(x, c, ctx, c_ctx, w_ada, b_ada, norm1_g, w_in, q_norm_g, k_norm_g, sgu_norm_g, sgu_w, sgu_b,
           w_out, norm2_g, router_w, w_gate, w_up, w_down, final_g):
    assert x.shape[0] == 1 and ctx.shape[0] == 1
    lat_len, ctx_len = x.shape[1], ctx.shape[1]
    assert ctx_len == TM and lat_len % TK == 0 and lat_len % (LANES * FFT_BATCH) == 0
    n = lat_len + ctx_len
    nl = lat_len // TM
    n1 = lat_len // LANES
    cap_l = CAPACITY_FACTOR * lat_len // N_EXPERTS
    cap_c = CAPACITY_FACTOR * ctx_len // N_EXPERTS
    capt = cap_l + cap_c
    assert cap_c <= LANES and capt % BF16_ROWS == 0 and N_EXPERTS % COMBINE_EXPERTS == 0
    ypad = -(-capt // COMBINE_BLK) * COMBINE_BLK
    depth = w_ada.shape[0]
    consts = _constants(lat_len, ctx_len)
    cos_t, sin_t = _rope_tables(lat_len, ctx_len)

    cvec = jnp.zeros((16, D_MODEL), F32).at[0].set(c[0]).at[1].set(c_ctx)
    mods = _mods(cvec, w_ada, b_ada)[:, :2].reshape(depth, 2, 6, D_MODEL)

    w_in_bf = w_in.astype(BF16)
    w_out_bf = w_out.astype(BF16)
    qg = jnp.tile(q_norm_g, (1, N_Q_HEADS)).reshape(depth, 1, ATTN_W)
    kg = jnp.tile(k_norm_g, (1, N_KV_HEADS)).reshape(depth, 1, KV_W)
    wcat = jnp.transpose(sgu_w, (0, 2, 1, 3)).reshape(depth, CHUNK, SGU_GROUPS * CHUNK).astype(BF16)
    sbias = jnp.repeat(jnp.transpose(sgu_b, (0, 2, 1)), SGU_GROUP_W, axis=2)
    rwt = jnp.transpose(router_w, (0, 2, 1))
    g1 = norm1_g.reshape(depth, 1, D_MODEL)
    g2 = norm2_g.reshape(depth, 1, D_MODEL)
    sgug = sgu_norm_g.reshape(depth, 1, SGU_W)
    fg = final_g.reshape(1, D_MODEL)

    t = jnp.concatenate([x[0], ctx[0]], axis=0)
    for layer in range(depth):
        final = layer == depth - 1
        qt, k, vt, sgu, g = _proj(layer, t, mods, g1, w_in_bf, cos_t, sin_t, qg, kg, consts['msq'], consts['rot'],
                                  sgug, wcat, sbias, consts['dftc'], nl)
        o = _attention(qt, k, vt, lat_len, ctx_len)
        y_lat, y_ctx = _fourier_positions(g, lat_len, ctx_len, consts)
        t, h2, aff = _outproj(layer, t, mods, o, sgu, y_lat, y_ctx, w_out_bf, g2, rwt, nl)

        aff_lat = aff[:, :lat_len].reshape(N_EXPERTS * n1, LANES)
        aff_ctx = jnp.pad(aff[:, lat_len:].reshape(N_EXPERTS, ctx_len // LANES, LANES),
                          ((0, 0), (0, CTX_ROUTE_ROWS - ctx_len // LANES), (0, 0))
                          ).reshape(N_EXPERTS * CTX_ROUTE_ROWS, LANES)
        slot_l, off_l, idx_l = _route(aff_lat, n1, cap_l, consts['ut'], consts['lt'], consts['slt_lat'])
        slot_c, _, idx_c = _route(aff_ctx, CTX_ROUTE_ROWS, cap_c, consts['ut'], consts['lt'], consts['slt_ctx'])

        idx = jnp.concatenate([idx_l[:, :cap_l], idx_c[:, :cap_c] + lat_len], axis=1)
        slot_c = slot_c.reshape(N_EXPERTS, CTX_ROUTE_ROWS * LANES)[:, :ctx_len]
        slot = jnp.concatenate([slot_l.reshape(N_EXPERTS, lat_len),
                                jnp.where(slot_c >= 0, slot_c + cap_l, -1)], axis=1)
        lo = off_l.reshape(N_EXPERTS, n1, LANES)[:, ::TM // LANES, 0]
        lo = jnp.concatenate([lo, jnp.full((N_EXPERTS, 1), cap_l, jnp.int32)], axis=1)
        hi = jnp.concatenate([lo[:, 1:], jnp.full((N_EXPERTS, 1), capt, jnp.int32)], axis=1)
        starts = jnp.minimum((lo // COMBINE_ALIGN) * COMBINE_ALIGN, ypad - COMBINE_BLK)
        extras = jnp.clip((hi - starts + COMBINE_BLK - 1) // COMBINE_BLK - 1, 0, COMBINE_MAX_BLKS - 1)

        xs = _gather(idx.reshape(-1), h2, capt)
        y = _ffn(layer, xs, w_gate, w_up, w_down)
        nt_out = nl if final else n // TM
        for e0 in range(0, N_EXPERTS, COMBINE_EXPERTS):
            last = e0 + COMBINE_EXPERTS == N_EXPERTS
            t = _combine(layer, starts.T.reshape(-1), extras.T.reshape(-1), t, mods, slot.T, aff.T, fg, y, nl,
                         e0, COMBINE_EXPERTS, nt_out, final and last)
    return t.reshape(1, lat_len, D_MODEL)
```

```python
import functools
import math

import numpy as np
import jax
import jax.numpy as jnp
from jax import lax
from jax.experimental import pallas as pl
from jax.experimental.pallas import tpu as pltpu

D_MODEL = 1024
DEPTH = 4
GRID_W = 64
HEAD_DIM = 64
ATTN_W = 512
N_Q_HEADS = 8
N_KV_HEADS = 2
Q_PER_KV = 4
KV_W = 128
AXIS_DIM = 32
ROPE_BASE = 10000.0
SGU_W = 256
SGU_GROUP_W = 64
SGU_GROUPS = 4
CHUNK = 128
FNET_W = 256
FNET_GROUP_W = 64
IN_W = 1536
N_EXPERTS = 16
CAPACITY_FACTOR = 2
EXPERT_FF = 2048
EPS = 1e-6

LANES = 128
TM = 256
TK = 2048
FF_TILE = 512
FFN_ROWS_MAX = 1040
COMBINE_BLK = 128
COMBINE_ALIGN = 64
COMBINE_MAX_BLKS = -(-(COMBINE_ALIGN - 1 + TM) // COMBINE_BLK)
COMBINE_EXPERTS = 8
BF16_ROWS = 16
GATHER_UNROLL = 8
CTX_ROUTE_ROWS = 16
V_ROWS = HEAD_DIM + 16
LOG2E = math.log2(math.e)
VMEM_LIMIT = 56 * 1024 * 1024

F32 = jnp.float32
BF16 = jnp.bfloat16


def _dot(a, b):
    return jnp.dot(a, b, preferred_element_type=F32)


def _dot_nt(a, b):
    return lax.dot_general(a, b, (((1,), (1,)), ((), ())), preferred_element_type=F32)


def _split(a):
    hi = a.astype(BF16)
    lo = (a - hi.astype(F32)).astype(BF16)
    return hi, lo


def _dot_x2(a, b_bf16):
    hi, lo = _split(a)
    return _dot(hi, b_bf16) + _dot(lo, b_bf16)


def _dot_x3(a, b):
    ah, al = _split(a)
    bh, bl = _split(b)
    return _dot(ah, bh) + (_dot(al, bh) + _dot(ah, bl))


def _sigmoid(a):
    return 1.0 / (1.0 + jnp.exp(-a))


def _params(sem, vmem=VMEM_LIMIT):
    return pltpu.CompilerParams(dimension_semantics=sem, vmem_limit_bytes=vmem)


def _blockdiag(block, n):
    k = block.shape[0]
    out = np.zeros((k * n, block.shape[1] * n), np.float64)
    for i in range(n):
        out[i * k:(i + 1) * k, i * block.shape[1]:(i + 1) * block.shape[1]] = block
    return out


def _dft_cs(n):
    a = 2.0 * np.pi * np.outer(np.arange(n), np.arange(n)) / n
    return np.cos(a), np.sin(a)


def _rot_matrix(width):
    h = AXIS_DIM // 2
    r = np.zeros((width, width), np.float64)
    for j in range(width):
        if (j % AXIS_DIM) < h:
            r[j + h, j] = -1.0
        else:
            r[j - h, j] = 1.0
    return r


def _rope_tables(L, C):
    rows = L // GRID_W
    r, col = jnp.meshgrid(jnp.arange(rows, dtype=F32), jnp.arange(GRID_W, dtype=F32), indexing='ij')
    inv = 1.0 / (ROPE_BASE ** (jnp.arange(0, AXIS_DIM, 2, dtype=F32) / AXIS_DIM))
    ar = r.reshape(-1, 1) * inv
    ac = col.reshape(-1, 1) * inv
    cos = jnp.concatenate([jnp.cos(ar), jnp.cos(ar), jnp.cos(ac), jnp.cos(ac)], axis=-1)
    sin = jnp.concatenate([jnp.sin(ar), jnp.sin(ar), jnp.sin(ac), jnp.sin(ac)], axis=-1)
    cos = jnp.concatenate([cos, jnp.ones((C, HEAD_DIM), F32)], axis=0)
    sin = jnp.concatenate([sin, jnp.zeros((C, HEAD_DIM), F32)], axis=0)
    return jnp.tile(cos, (1, 2)), jnp.tile(sin, (1, 2))


def _mods_kernel(cv_ref, w_ref, b_ref, o_ref):
    cv = cv_ref[...]
    s = cv * _sigmoid(cv)
    o_ref[...] = _dot_x3(s, w_ref[...]) + b_ref[...]


def _mods(cvec, w_ada, b_ada):
    depth, d, six_d = w_ada.shape
    rows = cvec.shape[0]
    tn = 1024
    return pl.pallas_call(
        _mods_kernel,
        grid=(depth, six_d // tn),
        in_specs=[pl.BlockSpec((rows, d), lambda l, j: (0, 0)),
                  pl.BlockSpec((None, d, tn), lambda l, j: (l, 0, j)),
                  pl.BlockSpec((None, 1, tn), lambda l, j: (l, 0, j))],
        out_specs=pl.BlockSpec((None, rows, tn), lambda l, j: (l, 0, j)),
        out_shape=jax.ShapeDtypeStruct((depth, rows, six_d), F32),
        compiler_params=_params(("arbitrary", "arbitrary")),
        name="adaln_mods",
    )(cvec, w_ada, b_ada.reshape(depth, 1, six_d))


def _head_norm_rope(t, gain, cos, sin, msq, rot):
    ms = _dot_x2(t * t, msq)
    tn = t * lax.rsqrt(ms + EPS) * gain
    return tn * cos + _dot(tn.astype(BF16), rot) * sin


def _gelu_tanh(z):
    return 0.5 * z * (1.0 + jnp.tanh(math.sqrt(2.0 / math.pi) * (z + 0.044715 * (z * z * z))))


def _proj_kernel(x_ref, mod_ref, g1_ref, win_ref, cos_ref, sin_ref, qg_ref, kg_ref, msq_ref, rot_ref,
                 sgug_ref, wcat_ref, sbias_ref, dft_ref,
                 qt_ref, k_ref, vt_ref, sgu_ref, g_ref):
    x = x_ref[...]
    shift = mod_ref[0:1, :]
    scale = mod_ref[1:2, :]
    h = x * lax.rsqrt(jnp.mean(x * x, axis=-1, keepdims=True) + EPS) * g1_ref[...]
    h = h * (1.0 + scale) + shift
    p = _dot(h.astype(BF16), win_ref[...])

    cos2 = cos_ref[...]
    sin2 = sin_ref[...]
    cos8 = jnp.concatenate([cos2] * 4, axis=1)
    sin8 = jnp.concatenate([sin2] * 4, axis=1)
    qo = _head_norm_rope(p[:, :ATTN_W], qg_ref[...], cos8, sin8, msq_ref[...], rot_ref[...])
    qot = (qo * (HEAD_DIM ** -0.5 * LOG2E)).T
    for hh in range(N_Q_HEADS):
        qt_ref[hh] = qot[hh * HEAD_DIM:(hh + 1) * HEAD_DIM, :].astype(BF16)

    ko = _head_norm_rope(p[:, ATTN_W:ATTN_W + KV_W], kg_ref[...], cos2, sin2,
                         msq_ref[:KV_W, :KV_W], rot_ref[:KV_W, :KV_W])
    vt = p[:, ATTN_W + KV_W:ATTN_W + 2 * KV_W].T
    ones_rows = jnp.where(lax.broadcasted_iota(jnp.int32, (V_ROWS - HEAD_DIM, LANES), 0) == 0, 1.0, 0.0)
    for hh in range(N_KV_HEADS):
        k_ref[hh] = ko[:, hh * HEAD_DIM:(hh + 1) * HEAD_DIM].astype(BF16)
        for cc in range(TM // LANES):
            vt_ref[hh, cc, :HEAD_DIM, :] = vt[hh * HEAD_DIM:(hh + 1) * HEAD_DIM,
                                              cc * LANES:(cc + 1) * LANES].astype(BF16)
            vt_ref[hh, cc, HEAD_DIM:, :] = ones_rows.astype(BF16)

    o3 = ATTN_W + 2 * KV_W
    ge = _gelu_tanh(p[:, o3:o3 + 2 * SGU_W])
    u = ge[:, :SGU_W]
    vv = ge[:, SGU_W:]
    vms = _dot_x2(vv * vv, msq_ref[:SGU_W, :SGU_W])
    vn = vv * lax.rsqrt(vms + EPS) * sgug_ref[...]
    lane_group = lax.broadcasted_iota(jnp.int32, (CHUNK, SGU_W), 1) // SGU_GROUP_W
    for cc in range(TM // CHUNK):
        vc = vn[cc * CHUNK:(cc + 1) * CHUNK]
        vstack = jnp.concatenate([jnp.where(lane_group == gg, vc, 0.0) for gg in range(SGU_GROUPS)],
                                 axis=0).astype(BF16)
        s = _dot(wcat_ref[...], vstack) + sbias_ref[...]
        sgu_ref[cc * CHUNK:(cc + 1) * CHUNK, :] = (u[cc * CHUNK:(cc + 1) * CHUNK] * s).astype(BF16)

    o4 = o3 + 2 * SGU_W
    g_ref[...] = _dot(p[:, o4:].astype(BF16), dft_ref[...]).astype(BF16)


def _proj(layer, t, mods, norm1_g, w_in_bf, cos_t, sin_t, qg, kg, msq, rot, sgug, wcat, sbias, dftc, nl):
    n = t.shape[0]
    nt = n // TM
    const = lambda *shape: pl.BlockSpec(shape, lambda i: (0,) * len(shape))
    lay = lambda *shape: pl.BlockSpec((None,) + shape, lambda i: (layer,) + (0,) * len(shape))
    return pl.pallas_call(
        _proj_kernel,
        grid=(nt,),
        in_specs=[pl.BlockSpec((TM, D_MODEL), lambda i: (i, 0)),
                  pl.BlockSpec((None, None, 6, D_MODEL), lambda i: (layer, i // nl, 0, 0)),
                  lay(1, D_MODEL),
                  lay(D_MODEL, IN_W),
                  pl.BlockSpec((TM, 2 * HEAD_DIM), lambda i: (i, 0)),
                  pl.BlockSpec((TM, 2 * HEAD_DIM), lambda i: (i, 0)),
                  lay(1, ATTN_W), lay(1, KV_W),
                  const(ATTN_W, ATTN_W), const(ATTN_W, ATTN_W),
                  lay(1, SGU_W), lay(CHUNK, SGU_GROUPS * CHUNK), lay(CHUNK, SGU_W),
                  const(FNET_W, 2 * FNET_W)],
        out_specs=[pl.BlockSpec((N_Q_HEADS, HEAD_DIM, TM), lambda i: (0, 0, i)),
                   pl.BlockSpec((N_KV_HEADS, TM, HEAD_DIM), lambda i: (0, i, 0)),
                   pl.BlockSpec((N_KV_HEADS, TM // LANES, V_ROWS, LANES), lambda i: (0, i, 0, 0)),
                   pl.BlockSpec((TM, SGU_W), lambda i: (i, 0)),
                   pl.BlockSpec((TM, 2 * FNET_W), lambda i: (i, 0))],
        out_shape=[jax.ShapeDtypeStruct((N_Q_HEADS, HEAD_DIM, n), BF16),
                   jax.ShapeDtypeStruct((N_KV_HEADS, n, HEAD_DIM), BF16),
                   jax.ShapeDtypeStruct((N_KV_HEADS, n // LANES, V_ROWS, LANES), BF16),
                   jax.ShapeDtypeStruct((n, SGU_W), BF16),
                   jax.ShapeDtypeStruct((n, 2 * FNET_W), BF16)],
        compiler_params=_params(("arbitrary",)),
        name="proj",
    )(t, mods, norm1_g, w_in_bf, cos_t, sin_t, qg, kg, msq, rot, sgug, wcat, sbias, dftc)


def _attn_kernel(qt_ref, k_ref, vt_ref, o_ref, *st_sc, n_lat_tiles, n_lat_chunks, ctx_len, lat_len):
    i = pl.program_id(1)
    bufs = (st_sc[:Q_PER_KV], st_sc[Q_PER_KV:])

    def scores(chunk, buf, g):
        start, nk = chunk
        buf[g][:nk, :] = _dot(k_ref[start:start + nk, :], qt_ref[g])

    def softmax_pv(chunk, buf, g, m_old, acc):
        start, nk = chunk
        vt = jnp.concatenate([vt_ref[start // LANES + c] for c in range(nk // LANES)], axis=1)
        st = buf[g][:nk, :]
        m_new = jnp.maximum(m_old, jnp.max(st, axis=0, keepdims=True))
        alpha = jnp.exp2(m_old - m_new)
        pt = jnp.exp2(st - m_new).astype(BF16)
        return m_new, alpha * acc + _dot(vt, pt)

    def run(chunks):
        carry = [(jnp.full((1, TM), -jnp.inf, F32), jnp.zeros((V_ROWS, TM), F32)) for _ in range(Q_PER_KV)]
        for g in range(Q_PER_KV):
            scores(chunks[0], bufs[0], g)
        for c, chunk in enumerate(chunks):
            cur, nxt = bufs[c % 2], bufs[(c + 1) % 2]
            for g in range(Q_PER_KV):
                if c + 1 < len(chunks):
                    scores(chunks[c + 1], nxt, g)
                carry[g] = softmax_pv(chunk, cur, g, *carry[g])
        o2 = jnp.concatenate([acc[:HEAD_DIM, :] / acc[HEAD_DIM:HEAD_DIM + 1, :] for _, acc in carry],
                             axis=0)
        o_ref[...] = o2.T.astype(BF16)

    ctx_chunk = (lat_len, ctx_len)

    @pl.when(i < n_lat_tiles)
    def _():
        run([(j * TK, TK) for j in range(n_lat_chunks)] + [ctx_chunk])

    @pl.when(i >= n_lat_tiles)
    def _():
        run([ctx_chunk])


def _attention(qt, k, vt, lat_len, ctx_len):
    n = k.shape[1]
    nt = n // TM
    kern = functools.partial(_attn_kernel, n_lat_tiles=lat_len // TM, n_lat_chunks=lat_len // TK,
                             ctx_len=ctx_len, lat_len=lat_len)
    return pl.pallas_call(
        kern,
        grid=(N_KV_HEADS, nt),
        in_specs=[pl.BlockSpec((Q_PER_KV, HEAD_DIM, TM), lambda h, i: (h, 0, i)),
                  pl.BlockSpec((None, n, HEAD_DIM), lambda h, i: (h, 0, 0)),
                  pl.BlockSpec((None, n // LANES, V_ROWS, LANES), lambda h, i: (h, 0, 0, 0))],
        out_specs=pl.BlockSpec((TM, Q_PER_KV * HEAD_DIM), lambda h, i: (i, h)),
        out_shape=jax.ShapeDtypeStruct((n, ATTN_W), BF16),
        scratch_shapes=[pltpu.VMEM((TK, TM), F32) for _ in range(2 * Q_PER_KV)],
        compiler_params=_params(("arbitrary", "arbitrary")),
        name="attention",
    )(qt, k, vt)


FFT_BATCH = 4


def _fft1_kernel(g_ref, cs_ref, twc_ref, tws_ref, o_ref):
    n1 = g_ref.shape[0]
    w = 2 * FNET_W
    for b in range(FFT_BATCH):
        xg = g_ref[:, b * w:(b + 1) * w]
        pq = _dot(cs_ref[...], xg)
        ar = pq[:n1, :FNET_W] + pq[n1:, FNET_W:]
        ai = pq[:n1, FNET_W:] - pq[n1:, :FNET_W]
        c = jnp.concatenate([twc_ref[b]] * (FNET_W // LANES), axis=1)
        s = jnp.concatenate([tws_ref[b]] * (FNET_W // LANES), axis=1)
        o_ref[:, b * w:b * w + FNET_W] = (ar * c + ai * s).astype(BF16)
        o_ref[:, b * w + FNET_W:(b + 1) * w] = (ai * c - ar * s).astype(BF16)


def _fft2_kernel(t_ref, cs_ref, o_ref):
    for b in range(FFT_BATCH):
        tb = t_ref[b * LANES:(b + 1) * LANES, :]
        st = jnp.concatenate([tb[:, :FNET_W], tb[:, FNET_W:]], axis=0)
        o_ref[:, b * FNET_W:(b + 1) * FNET_W] = _dot(cs_ref[...], st).astype(BF16)


def _fftc_kernel(g_ref, cs_ref, o_ref):
    gb = g_ref[...]
    st = jnp.concatenate([gb[:, :FNET_W], gb[:, FNET_W:]], axis=0)
    o_ref[...] = _dot(cs_ref[...], st).astype(BF16)


def _fourier_positions(g, lat_len, ctx_len, consts):
    n1 = lat_len // LANES
    w = 2 * FNET_W
    g_lat = g[:lat_len].reshape(n1, LANES * w)
    t = pl.pallas_call(
        _fft1_kernel,
        grid=(LANES // FFT_BATCH,),
        in_specs=[pl.BlockSpec((n1, FFT_BATCH * w), lambda j: (0, j)),
                  pl.BlockSpec((2 * n1, n1), lambda j: (0, 0)),
                  pl.BlockSpec((FFT_BATCH, n1, LANES), lambda j: (j, 0, 0)),
                  pl.BlockSpec((FFT_BATCH, n1, LANES), lambda j: (j, 0, 0))],
        out_specs=pl.BlockSpec((n1, FFT_BATCH * w), lambda j: (0, j)),
        out_shape=jax.ShapeDtypeStruct((n1, LANES * w), BF16),
        compiler_params=_params(("arbitrary",)),
        name="fft_stage1",
    )(g_lat, consts['cs1'], consts['twc'], consts['tws'])
    y_lat = pl.pallas_call(
        _fft2_kernel,
        grid=(n1 // FFT_BATCH,),
        in_specs=[pl.BlockSpec((FFT_BATCH * LANES, w), lambda j: (j, 0)),
                  pl.BlockSpec((LANES, 2 * LANES), lambda j: (0, 0))],
        out_specs=pl.BlockSpec((LANES, FFT_BATCH * FNET_W), lambda j: (0, j)),
        out_shape=jax.ShapeDtypeStruct((LANES, n1 * FNET_W), BF16),
        compiler_params=_params(("arbitrary",)),
        name="fft_stage2",
    )(t.reshape(n1 * LANES, w), consts['cs2'])
    y_ctx = pl.pallas_call(
        _fftc_kernel,
        grid=(1,),
        in_specs=[pl.BlockSpec((ctx_len, w), lambda j: (lat_len // ctx_len, 0)),
                  pl.BlockSpec((ctx_len, 2 * ctx_len), lambda j: (0, 0))],
        out_specs=pl.BlockSpec((ctx_len, FNET_W), lambda j: (0, 0)),
        out_shape=jax.ShapeDtypeStruct((ctx_len, FNET_W), BF16),
        compiler_params=_params(("arbitrary",)),
        name="fft_ctx",
    )(g, consts['csc'])
    return y_lat.reshape(lat_len, FNET_W), y_ctx


def _outproj_kernel(x_ref, mod_ref, o_ref, sgu_ref, yl_ref, yc_ref, wout_ref, g2_ref, rw_ref,
                    xo_ref, h2_ref, aff_ref, *, n_lat_tiles):
    i = pl.program_id(0)
    yf = jnp.where(i < n_lat_tiles, yl_ref[...], yc_ref[...])
    mix = _dot(o_ref[...], wout_ref[:ATTN_W, :])
    mix = mix + _dot(sgu_ref[...], wout_ref[ATTN_W:ATTN_W + SGU_W, :])
    mix = mix + _dot(yf, wout_ref[ATTN_W + SGU_W:, :])
    x = x_ref[...] + mod_ref[2:3, :] * mix
    xo_ref[...] = x
    h = x * lax.rsqrt(jnp.mean(x * x, axis=-1, keepdims=True) + EPS) * g2_ref[...]
    h = h * (1.0 + mod_ref[4:5, :]) + mod_ref[3:4, :]
    h2_ref[...] = h
    logits = _dot_x3(h, rw_ref[...])
    expert_lane = lax.broadcasted_iota(jnp.int32, logits.shape, 1) < N_EXPERTS
    logits = jnp.where(expert_lane, logits, -jnp.inf)
    z = jnp.exp(logits - jnp.max(logits, axis=-1, keepdims=True))
    aff_ref[...] = z / jnp.sum(z, axis=-1, keepdims=True)


def _outproj(layer, t, mods, o, sgu, y_lat, y_ctx, w_out_bf, norm2_g, rw, nl):
    n = t.shape[0]
    nt = n // TM
    lay = lambda *shape: pl.BlockSpec((None,) + shape, lambda i: (layer,) + (0,) * len(shape))
    return pl.pallas_call(
        functools.partial(_outproj_kernel, n_lat_tiles=nl),
        grid=(nt,),
        in_specs=[pl.BlockSpec((TM, D_MODEL), lambda i: (i, 0)),
                  pl.BlockSpec((None, None, 6, D_MODEL), lambda i: (layer, i // nl, 0, 0)),
                  pl.BlockSpec((TM, ATTN_W), lambda i: (i, 0)),
                  pl.BlockSpec((TM, SGU_W), lambda i: (i, 0)),
                  pl.BlockSpec((TM, FNET_W), lambda i: (jnp.minimum(i, nl - 1), 0)),
                  pl.BlockSpec((TM, FNET_W), lambda i: (0, 0)),
                  lay(D_MODEL, D_MODEL), lay(1, D_MODEL), lay(D_MODEL, LANES)],
        out_specs=[pl.BlockSpec((TM, D_MODEL), lambda i: (i, 0)),
                   pl.BlockSpec((TM, D_MODEL), lambda i: (i, 0)),
                   pl.BlockSpec((TM, LANES), lambda i: (i, 0))],
        out_shape=[jax.ShapeDtypeStruct((n, D_MODEL), F32),
                   jax.ShapeDtypeStruct((n, D_MODEL), F32),
                   jax.ShapeDtypeStruct((n, LANES), F32)],
        compiler_params=_params(("arbitrary",)),
        name="outproj_router",
    )(t, mods, o, sgu, y_lat, y_ctx, w_out_bf, norm2_g, rw)


def _route_kernel(a_ref, ut_ref, lt_ref, slt_ref, slot_ref, off_ref, idx_ref, *, rows, cap, capw):
    e_n = N_EXPERTS
    a3 = a_ref[...].reshape(e_n, rows, LANES)

    def count(mask):
        c = jnp.sum(jnp.where(mask, 1.0, 0.0), axis=1, keepdims=True)
        return jnp.sum(c, axis=2, keepdims=True)

    def as_float(bits):
        return lax.bitcast_convert_type(bits, F32)

    def search(it, thr):
        cand = thr | jnp.left_shift(jnp.int32(1), 30 - it)
        return jnp.where(count(a3 >= as_float(cand)) >= cap, cand, thr)

    thr = lax.fori_loop(0, 31, search, jnp.zeros((e_n, 1, 1), jnp.int32))
    above = as_float(thr + 1)
    gt = a3 >= above
    eq = jnp.logical_and(a3 >= as_float(thr), a3 < above)
    need = cap - count(gt)

    def ranks(mask3):
        m = jnp.where(mask3, 1.0, 0.0).reshape(e_n * rows, LANES)
        cin = _dot(m.astype(BF16), ut_ref[...])
        tot = cin[:, LANES - 1:LANES]
        totb = jnp.broadcast_to(tot, (e_n * rows, LANES)).astype(BF16)
        offs = [_dot(slt_ref[...], totb[e * rows:(e + 1) * rows]) for e in range(e_n)]
        off = jnp.concatenate(offs, axis=0)
        return m, cin, off

    m_eq, cin_eq, off_eq = ranks(eq)
    eq_rank = (off_eq + cin_eq - m_eq).reshape(e_n, rows, LANES)
    sel = gt | (eq & (eq_rank < need))
    m_sel, cin_sel, off_sel = ranks(sel)
    pos = off_sel + cin_sel - m_sel
    slot_ref[...] = jnp.where(m_sel > 0.5, pos, -1.0).astype(jnp.int32)
    off_ref[...] = off_sel.astype(jnp.int32)

    r_lane = lax.broadcasted_iota(jnp.int32, (1, capw), 1).astype(F32)
    row_iota = lax.broadcasted_iota(jnp.int32, (rows, capw), 0).astype(F32)
    for e in range(e_n):
        sl = slice(e * rows, (e + 1) * rows)
        off_e = off_sel[sl, 0:1]
        end_e = off_e + cin_sel[sl, LANES - 1:LANES]
        a_of_r = jnp.sum(jnp.where(end_e <= r_lane, 1.0, 0.0), axis=0, keepdims=True)
        onehot = row_iota == a_of_r
        off_at = jnp.sum(jnp.where(onehot, off_e, 0.0), axis=0, keepdims=True)
        cin_t = _dot_nt(lt_ref[...], m_sel[sl].astype(BF16))
        crow = _dot(cin_t.astype(BF16), jnp.where(onehot, 1.0, 0.0).astype(BF16))
        b_of_r = jnp.sum(jnp.where(crow <= r_lane - off_at, 1.0, 0.0), axis=0, keepdims=True)
        idx_ref[e:e + 1, :] = (a_of_r * LANES + b_of_r).astype(jnp.int32)


def _route(aff2, rows, cap, ut, lt, slt):
    capw = max(cap, LANES)
    full = lambda *shape: pl.BlockSpec(shape, lambda i: (0,) * len(shape))
    return pl.pallas_call(
        functools.partial(_route_kernel, rows=rows, cap=cap, capw=capw),
        grid=(1,),
        in_specs=[full(N_EXPERTS * rows, LANES), full(LANES, LANES), full(LANES, LANES), full(rows, rows)],
        out_specs=[full(N_EXPERTS * rows, LANES), full(N_EXPERTS * rows, LANES), full(N_EXPERTS, capw)],
        out_shape=[jax.ShapeDtypeStruct((N_EXPERTS * rows, LANES), jnp.int32),
                   jax.ShapeDtypeStruct((N_EXPERTS * rows, LANES), jnp.int32),
                   jax.ShapeDtypeStruct((N_EXPERTS, capw), jnp.int32)],
        compiler_params=_params(("arbitrary",)),
        name="route",
    )(aff2, ut, lt, slt)


def _gather_kernel(idx_ref, h_hbm, o_ref, buf, sem, *, capt, rb):
    base = pl.program_id(0) * capt + pl.program_id(1) * rb

    def issue(r, carry):
        tok = idx_ref[base + r]
        pltpu.make_async_copy(h_hbm.at[pl.ds(tok, 1), :], buf.at[pl.ds(r, 1), :], sem).start()
        return carry

    lax.fori_loop(0, rb, issue, 0, unroll=GATHER_UNROLL)
    pltpu.make_async_copy(h_hbm.at[pl.ds(0, rb), :], buf, sem).wait()
    o_ref[...] = buf[...].astype(BF16)


def _row_block(capt, limit=512):
    return max(r for r in range(BF16_ROWS, limit + 1, BF16_ROWS) if capt % r == 0)


def _gather(idx_flat, h2, capt):
    rb = _row_block(capt)
    return pl.pallas_call(
        functools.partial(_gather_kernel, capt=capt, rb=rb),
        grid_spec=pltpu.PrefetchScalarGridSpec(
            num_scalar_prefetch=1,
            grid=(N_EXPERTS, capt // rb),
            in_specs=[pl.BlockSpec(memory_space=pl.ANY)],
            out_specs=pl.BlockSpec((None, rb, D_MODEL), lambda e, r, idx: (e, r, 0)),
            scratch_shapes=[pltpu.VMEM((rb, D_MODEL), F32), pltpu.SemaphoreType.DMA(())]),
        out_shape=jax.ShapeDtypeStruct((N_EXPERTS, capt, D_MODEL), BF16),
        compiler_params=_params(("arbitrary", "arbitrary")),
        name="expert_gather",
    )(idx_flat, h2)


def _ffn_kernel(xs_ref, wg_ref, wu_ref, wd_ref, y_ref, acc_ref, *, rc):
    j = pl.program_id(1)
    wg = wg_ref[...].astype(BF16)
    wu = wu_ref[...].astype(BF16)
    wd = wd_ref[...].astype(BF16)
    capt = xs_ref.shape[0]

    @pl.when(j == 0)
    def _():
        acc_ref[...] = jnp.zeros(acc_ref.shape, F32)

    for r in range(capt // rc):
        sl = slice(r * rc, (r + 1) * rc)
        xs = xs_ref[sl, :]
        a = _dot(xs, wg)
        u = _dot(xs, wu)
        hmid = (a * _sigmoid(a) * u).astype(BF16)
        acc_ref[sl, :] += _dot(hmid, wd)

    @pl.when(j == pl.num_programs(1) - 1)
    def _():
        y_ref[...] = acc_ref[...].astype(BF16)


def _ffn(layer, xs, w_gate, w_up, w_down):
    capt = xs.shape[1]
    rc = _row_block(capt, FFN_ROWS_MAX)
    return pl.pallas_call(
        functools.partial(_ffn_kernel, rc=rc),
        grid=(N_EXPERTS, EXPERT_FF // FF_TILE),
        in_specs=[pl.BlockSpec((None, capt, D_MODEL), lambda e, j: (e, 0, 0)),
                  pl.BlockSpec((None, None, D_MODEL, FF_TILE), lambda e, j: (layer, e, 0, j)),
                  pl.BlockSpec((None, None, D_MODEL, FF_TILE), lambda e, j: (layer, e, 0, j)),
                  pl.BlockSpec((None, None, FF_TILE, D_MODEL), lambda e, j: (layer, e, j, 0))],
        out_specs=pl.BlockSpec((None, capt, D_MODEL), lambda e, j: (e, 0, 0)),
        out_shape=jax.ShapeDtypeStruct((N_EXPERTS, capt, D_MODEL), BF16),
        scratch_shapes=[pltpu.VMEM((capt, D_MODEL), F32)],
        compiler_params=_params(("arbitrary", "arbitrary")),
        name="expert_ffn",
    )(xs, w_gate, w_up, w_down)


def _combine_kernel(start_ref, extra_ref, x_ref, mod_ref, slot_ref, aff_ref, fg_ref, y_hbm, o_ref,
                    y_sc, acc_sc, sem, *, e0, ne, capt, final):
    i = pl.program_id(0)

    @pl.when(i == 0)
    def _():
        y_sc[:, capt:, :] = jnp.zeros((ne, y_sc.shape[1] - capt, D_MODEL), BF16)
        cp = pltpu.make_async_copy(y_hbm.at[pl.ds(e0, ne)], y_sc.at[:, pl.ds(0, capt), :], sem)
        cp.start()
        cp.wait()

    lane = lax.broadcasted_iota(jnp.int32, (TM, COMBINE_BLK), 1)

    def block(e, row0):
        rel = slot_ref[:, e0 + e:e0 + e + 1] - row0
        onehot = jnp.where(rel == lane, aff_ref[:, e0 + e:e0 + e + 1], 0.0).astype(BF16)
        return _dot(onehot, y_sc[e, pl.ds(row0, COMBINE_BLK), :])

    def start(e):
        return pl.multiple_of(start_ref[i * N_EXPERTS + e0 + e], COMBINE_ALIGN)

    acc = block(0, start(0))
    for e in range(1, ne):
        acc = acc + block(e, start(e))
    acc_sc[...] = acc
    for e in range(ne):
        for kb in range(1, COMBINE_MAX_BLKS):
            @pl.when(extra_ref[i * N_EXPERTS + e0 + e] >= kb)
            def _():
                acc_sc[...] += block(e, start(e) + kb * COMBINE_BLK)

    x = x_ref[...] + mod_ref[5:6, :] * acc_sc[...]
    if final:
        x = x * lax.rsqrt(jnp.mean(x * x, axis=-1, keepdims=True) + EPS) * fg_ref[...]
    o_ref[...] = x


def _combine(layer, starts, extras, t, mods, slot_t, aff_t, final_g, y, nl, e0, ne, nt, final):
    capt = y.shape[1]
    ypad = -(-capt // COMBINE_BLK) * COMBINE_BLK
    return pl.pallas_call(
        functools.partial(_combine_kernel, e0=e0, ne=ne, capt=capt, final=final),
        grid_spec=pltpu.PrefetchScalarGridSpec(
            num_scalar_prefetch=2,
            grid=(nt,),
            in_specs=[pl.BlockSpec((TM, D_MODEL), lambda i, s, x: (i, 0)),
                      pl.BlockSpec((None, None, 6, D_MODEL), lambda i, s, x: (layer, i // nl, 0, 0)),
                      pl.BlockSpec((TM, N_EXPERTS), lambda i, s, x: (i, 0)),
                      pl.BlockSpec((TM, LANES), lambda i, s, x: (i, 0)),
                      pl.BlockSpec((1, D_MODEL), lambda i, s, x: (0, 0)),
                      pl.BlockSpec(memory_space=pl.ANY)],
            out_specs=pl.BlockSpec((TM, D_MODEL), lambda i, s, x: (i, 0)),
            scratch_shapes=[pltpu.VMEM((ne, ypad, D_MODEL), BF16), pltpu.VMEM((TM, D_MODEL), F32),
                            pltpu.SemaphoreType.DMA(())]),
        out_shape=jax.ShapeDtypeStruct((nt * TM, D_MODEL), F32),
        compiler_params=_params(("arbitrary",)),
        name="combine",
    )(starts, extras, t, mods, slot_t, aff_t, final_g, y)


def _constants(lat_len, ctx_len):
    n1 = lat_len // LANES
    c64, s64 = _dft_cs(FNET_GROUP_W)
    groups = FNET_W // FNET_GROUP_W
    dftc = np.concatenate([_blockdiag(c64, groups), -_blockdiag(s64, groups)], axis=1) / math.sqrt(FNET_GROUP_W)
    c1, s1 = _dft_cs(n1)
    cs1 = np.concatenate([c1, s1], axis=0) / math.sqrt(n1)
    c2, s2 = _dft_cs(LANES)
    cs2 = np.concatenate([c2, s2], axis=1) / math.sqrt(LANES)
    cc, sc = _dft_cs(ctx_len)
    csc = np.concatenate([cc, sc], axis=1) / math.sqrt(ctx_len)
    n2 = jnp.arange(LANES, dtype=F32)[:, None]
    k1 = jnp.arange(n1, dtype=F32)[None, :]
    theta = (n2 * k1) * (2.0 * math.pi / lat_len)
    bcast = lambda a: jnp.broadcast_to(a[:, :, None], (LANES, n1, LANES))
    tri = np.triu(np.ones((LANES, LANES)))
    dft = lambda a: jnp.asarray(a, F32).astype(BF16)
    return dict(
        dftc=dft(dftc), cs1=dft(cs1), cs2=dft(cs2), csc=dft(csc),
        twc=bcast(jnp.cos(theta)), tws=bcast(jnp.sin(theta)),
        msq=jnp.asarray(_blockdiag(np.full((HEAD_DIM, HEAD_DIM), 1.0 / HEAD_DIM), N_Q_HEADS), BF16),
        rot=jnp.asarray(_rot_matrix(ATTN_W), BF16),
        ut=jnp.asarray(tri, BF16), lt=jnp.asarray(tri.T, BF16),
        slt_lat=jnp.asarray(np.tril(np.ones((n1, n1)), -1), BF16),
        slt_ctx=jnp.asarray(np.tril(np.ones((CTX_ROUTE_ROWS, CTX_ROUTE_ROWS)), -1), BF16),
    )


def kernel(x, c, ctx, c_ctx, w_ada, b_ada, norm1_g, w_in, q_norm_g, k_norm_g, sgu_norm_g, sgu_w, sgu_b,
           w_out, norm2_g, router_w, w_gate, w_up, w_down, final_g):
    assert x.shape[0] == 1 and ctx.shape[0] == 1
    lat_len, ctx_len = x.shape[1], ctx.shape[1]
    assert ctx_len == TM and lat_len % TK == 0 and lat_len % (LANES * FFT_BATCH) == 0
    n = lat_len + ctx_len
    nl = lat_len // TM
    n1 = lat_len // LANES
    cap_l = CAPACITY_FACTOR * lat_len // N_EXPERTS
    cap_c = CAPACITY_FACTOR * ctx_len // N_EXPERTS
    capt = cap_l + cap_c
    assert cap_c <= LANES and capt % BF16_ROWS == 0 and N_EXPERTS % COMBINE_EXPERTS == 0
    ypad = -(-capt // COMBINE_BLK) * COMBINE_BLK
    depth = w_ada.shape[0]
    consts = _constants(lat_len, ctx_len)
    cos_t, sin_t = _rope_tables(lat_len, ctx_len)

    cvec = jnp.zeros((16, D_MODEL), F32).at[0].set(c[0]).at[1].set(c_ctx)
    mods = _mods(cvec, w_ada, b_ada)[:, :2].reshape(depth, 2, 6, D_MODEL)

    w_in_bf = w_in.astype(BF16)
    w_out_bf = w_out.astype(BF16)
    qg = jnp.tile(q_norm_g, (1, N_Q_HEADS)).reshape(depth, 1, ATTN_W)
    kg = jnp.tile(k_norm_g, (1, N_KV_HEADS)).reshape(depth, 1, KV_W)
    wcat = jnp.transpose(sgu_w, (0, 2, 1, 3)).reshape(depth, CHUNK, SGU_GROUPS * CHUNK).astype(BF16)
    sbias = jnp.repeat(jnp.transpose(sgu_b, (0, 2, 1)), SGU_GROUP_W, axis=2)
    rw = jnp.pad(router_w, ((0, 0), (0, 0), (0, LANES - N_EXPERTS)))
    g1 = norm1_g.reshape(depth, 1, D_MODEL)
    g2 = norm2_g.reshape(depth, 1, D_MODEL)
    sgug = sgu_norm_g.reshape(depth, 1, SGU_W)
    fg = final_g.reshape(1, D_MODEL)

    t = jnp.concatenate([x[0], ctx[0]], axis=0)
    for layer in range(depth):
        final = layer == depth - 1
        qt, k, vt, sgu, g = _proj(layer, t, mods, g1, w_in_bf, cos_t, sin_t, qg, kg, consts['msq'], consts['rot'],
                                  sgug, wcat, sbias, consts['dftc'], nl)
        o = _attention(qt, k, vt, lat_len, ctx_len)
        y_lat, y_ctx = _fourier_positions(g, lat_len, ctx_len, consts)
        t, h2, aff_tok = _outproj(layer, t, mods, o, sgu, y_lat, y_ctx, w_out_bf, g2, rw, nl)
        aff = aff_tok[:, :N_EXPERTS].T

        aff_lat = aff[:, :lat_len].reshape(N_EXPERTS * n1, LANES)
        aff_ctx = jnp.pad(aff[:, lat_len:].reshape(N_EXPERTS, ctx_len // LANES, LANES),
                          ((0, 0), (0, CTX_ROUTE_ROWS - ctx_len // LANES), (0, 0))
                          ).reshape(N_EXPERTS * CTX_ROUTE_ROWS, LANES)
        slot_l, off_l, idx_l = _route(aff_lat, n1, cap_l, consts['ut'], consts['lt'], consts['slt_lat'])
        slot_c, _, idx_c = _route(aff_ctx, CTX_ROUTE_ROWS, cap_c, consts['ut'], consts['lt'], consts['slt_ctx'])

        idx = jnp.concatenate([idx_l[:, :cap_l], idx_c[:, :cap_c] + lat_len], axis=1)
        slot_c = slot_c.reshape(N_EXPERTS, CTX_ROUTE_ROWS * LANES)[:, :ctx_len]
        slot = jnp.concatenate([slot_l.reshape(N_EXPERTS, lat_len),
                                jnp.where(slot_c >= 0, slot_c + cap_l, -1)], axis=1)
        lo = off_l.reshape(N_EXPERTS, n1, LANES)[:, ::TM // LANES, 0]
        lo = jnp.concatenate([lo, jnp.full((N_EXPERTS, 1), cap_l, jnp.int32)], axis=1)
        hi = jnp.concatenate([lo[:, 1:], jnp.full((N_EXPERTS, 1), capt, jnp.int32)], axis=1)
        starts = jnp.minimum((lo // COMBINE_ALIGN) * COMBINE_ALIGN, ypad - COMBINE_BLK)
        extras = jnp.clip((hi - starts + COMBINE_BLK - 1) // COMBINE_BLK - 1, 0, COMBINE_MAX_BLKS - 1)

        xs = _gather(idx.reshape(-1), h2, capt)
        y = _ffn(layer, xs, w_gate, w_up, w_down)
        nt_out = nl if final else n // TM
        for e0 in range(0, N_EXPERTS, COMBINE_EXPERTS):
            last = e0 + COMBINE_EXPERTS == N_EXPERTS
            t = _combine(layer, starts.T.reshape(-1), extras.T.reshape(-1), t, mods, slot.T, aff_tok, fg, y, nl,
                         e0, COMBINE_EXPERTS, nt_out, final and last)
    return t.reshape(1, lat_len, D_MODEL)
```

```python
import functools
import math

import numpy as np
import jax
import jax.numpy as jnp
from jax import lax
from jax.experimental import pallas as pl
from jax.experimental.pallas import tpu as pltpu

D_MODEL = 1024
DEPTH = 4
GRID_W = 64
HEAD_DIM = 64
ATTN_W = 512
N_Q_HEADS = 8
N_KV_HEADS = 2
Q_PER_KV = 4
KV_W = 128
AXIS_DIM = 32
ROPE_BASE = 10000.0
SGU_W = 256
SGU_GROUP_W = 64
SGU_GROUPS = 4
CHUNK = 128
FNET_W = 256
FNET_GROUP_W = 64
IN_W = 1536
N_EXPERTS = 16
CAPACITY_FACTOR = 2
EXPERT_FF = 2048
EPS = 1e-6

LANES = 128
TM = 256
TK = 2048
FF_TILE = 512
FFN_ROWS_MAX = 1040
COMBINE_BLK = 128
COMBINE_ALIGN = 64
COMBINE_MAX_BLKS = -(-(COMBINE_ALIGN - 1 + TM) // COMBINE_BLK)
COMBINE_EXPERTS = 8
BF16_ROWS = 16
GATHER_UNROLL = 8
CTX_ROUTE_ROWS = 16
V_ROWS = HEAD_DIM + BF16_ROWS
LOG2E = math.log2(math.e)
VMEM_LIMIT = 56 * 1024 * 1024

F32 = jnp.float32
BF16 = jnp.bfloat16


def _dot(a, b):
    return jnp.dot(a, b, preferred_element_type=F32)


def _dot_nt(a, b):
    return lax.dot_general(a, b, (((1,), (1,)), ((), ())), preferred_element_type=F32)


def _split(a):
    hi = a.astype(BF16)
    lo = (a - hi.astype(F32)).astype(BF16)
    return hi, lo


def _dot_x2(a, b_bf16):
    hi, lo = _split(a)
    return _dot(hi, b_bf16) + _dot(lo, b_bf16)


def _dot_x3(a, b):
    ah, al = _split(a)
    bh, bl = _split(b)
    return _dot(ah, bh) + (_dot(al, bh) + _dot(ah, bl))


def _sigmoid(a):
    return 1.0 / (1.0 + jnp.exp(-a))


def _params(sem, vmem=VMEM_LIMIT):
    return pltpu.CompilerParams(dimension_semantics=sem, vmem_limit_bytes=vmem)


def _blockdiag(block, n):
    k = block.shape[0]
    out = np.zeros((k * n, block.shape[1] * n), np.float64)
    for i in range(n):
        out[i * k:(i + 1) * k, i * block.shape[1]:(i + 1) * block.shape[1]] = block
    return out


def _dft_cs(n):
    a = 2.0 * np.pi * np.outer(np.arange(n), np.arange(n)) / n
    return np.cos(a), np.sin(a)


def _rot_matrix(width):
    h = AXIS_DIM // 2
    r = np.zeros((width, width), np.float64)
    for j in range(width):
        if (j % AXIS_DIM) < h:
            r[j + h, j] = -1.0
        else:
            r[j - h, j] = 1.0
    return r


def _rope_tables(L, C):
    rows = L // GRID_W
    r, col = jnp.meshgrid(jnp.arange(rows, dtype=F32), jnp.arange(GRID_W, dtype=F32), indexing='ij')
    inv = 1.0 / (ROPE_BASE ** (jnp.arange(0, AXIS_DIM, 2, dtype=F32) / AXIS_DIM))
    ar = r.reshape(-1, 1) * inv
    ac = col.reshape(-1, 1) * inv
    cos = jnp.concatenate([jnp.cos(ar), jnp.cos(ar), jnp.cos(ac), jnp.cos(ac)], axis=-1)
    sin = jnp.concatenate([jnp.sin(ar), jnp.sin(ar), jnp.sin(ac), jnp.sin(ac)], axis=-1)
    cos = jnp.concatenate([cos, jnp.ones((C, HEAD_DIM), F32)], axis=0)
    sin = jnp.concatenate([sin, jnp.zeros((C, HEAD_DIM), F32)], axis=0)
    return jnp.tile(cos, (1, 2)), jnp.tile(sin, (1, 2))


def _mods_kernel(cv_ref, w_ref, b_ref, o_ref):
    cv = cv_ref[...]
    s = cv * _sigmoid(cv)
    o_ref[...] = _dot_x3(s, w_ref[...]) + b_ref[...]


def _mods(cvec, w_ada, b_ada):
    depth, d, six_d = w_ada.shape
    rows = cvec.shape[0]
    tn = 1024
    return pl.pallas_call(
        _mods_kernel,
        grid=(depth, six_d // tn),
        in_specs=[pl.BlockSpec((rows, d), lambda l, j: (0, 0)),
                  pl.BlockSpec((None, d, tn), lambda l, j: (l, 0, j)),
                  pl.BlockSpec((None, 1, tn), lambda l, j: (l, 0, j))],
        out_specs=pl.BlockSpec((None, rows, tn), lambda l, j: (l, 0, j)),
        out_shape=jax.ShapeDtypeStruct((depth, rows, six_d), F32),
        compiler_params=_params(("arbitrary", "arbitrary")),
        name="adaln_mods",
    )(cvec, w_ada, b_ada.reshape(depth, 1, six_d))


def _head_norm_rope(t, gain, cos, sin, msq, rot):
    ms = _dot_x2(t * t, msq)
    tn = t * lax.rsqrt(ms + EPS) * gain
    return tn * cos + _dot(tn.astype(BF16), rot) * sin


def _gelu_tanh(z):
    return 0.5 * z * (1.0 + jnp.tanh(math.sqrt(2.0 / math.pi) * (z + 0.044715 * (z * z * z))))


def _proj_kernel(x_ref, mod_ref, g1_ref, win_ref, cos_ref, sin_ref, qg_ref, kg_ref, msq_ref, rot_ref,
                 sgug_ref, wcat_ref, sbias_ref, dft_ref,
                 qt_ref, k_ref, vt_ref, sgu_ref, g_ref):
    x = x_ref[...]
    shift = mod_ref[0:1, :]
    scale = mod_ref[1:2, :]
    h = x * lax.rsqrt(jnp.mean(x * x, axis=-1, keepdims=True) + EPS) * g1_ref[...]
    h = h * (1.0 + scale) + shift
    p = _dot(h.astype(BF16), win_ref[...])

    cos2 = cos_ref[...]
    sin2 = sin_ref[...]
    cos8 = jnp.concatenate([cos2] * 4, axis=1)
    sin8 = jnp.concatenate([sin2] * 4, axis=1)
    qo = _head_norm_rope(p[:, :ATTN_W], qg_ref[...], cos8, sin8, msq_ref[...], rot_ref[...])
    qot = (qo * (HEAD_DIM ** -0.5 * LOG2E)).T
    for hh in range(N_Q_HEADS):
        qt_ref[hh] = qot[hh * HEAD_DIM:(hh + 1) * HEAD_DIM, :].astype(BF16)

    ko = _head_norm_rope(p[:, ATTN_W:ATTN_W + KV_W], kg_ref[...], cos2, sin2,
                         msq_ref[:KV_W, :KV_W], rot_ref[:KV_W, :KV_W])
    vt = p[:, ATTN_W + KV_W:ATTN_W + 2 * KV_W].T
    ones_rows = jnp.where(lax.broadcasted_iota(jnp.int32, (V_ROWS - HEAD_DIM, LANES), 0) == 0, 1.0, 0.0)
    for hh in range(N_KV_HEADS):
        k_ref[hh] = ko[:, hh * HEAD_DIM:(hh + 1) * HEAD_DIM].astype(BF16)
        for cc in range(TM // LANES):
            vt_ref[hh, cc, :HEAD_DIM, :] = vt[hh * HEAD_DIM:(hh + 1) * HEAD_DIM,
                                              cc * LANES:(cc + 1) * LANES].astype(BF16)
            vt_ref[hh, cc, HEAD_DIM:, :] = ones_rows.astype(BF16)

    o3 = ATTN_W + 2 * KV_W
    ge = _gelu_tanh(p[:, o3:o3 + 2 * SGU_W])
    u = ge[:, :SGU_W]
    vv = ge[:, SGU_W:]
    vms = _dot_x2(vv * vv, msq_ref[:SGU_W, :SGU_W])
    vn = vv * lax.rsqrt(vms + EPS) * sgug_ref[...]
    lane_group = lax.broadcasted_iota(jnp.int32, (CHUNK, SGU_W), 1) // SGU_GROUP_W
    for cc in range(TM // CHUNK):
        vc = vn[cc * CHUNK:(cc + 1) * CHUNK]
        vstack = jnp.concatenate([jnp.where(lane_group == gg, vc, 0.0) for gg in range(SGU_GROUPS)],
                                 axis=0).astype(BF16)
        s = _dot(wcat_ref[...], vstack) + sbias_ref[...]
        sgu_ref[cc * CHUNK:(cc + 1) * CHUNK, :] = (u[cc * CHUNK:(cc + 1) * CHUNK] * s).astype(BF16)

    o4 = o3 + 2 * SGU_W
    g_ref[...] = _dot(p[:, o4:].astype(BF16), dft_ref[...]).astype(BF16)


def _proj(layer, t, mods, norm1_g, w_in_bf, cos_t, sin_t, qg, kg, msq, rot, sgug, wcat, sbias, dftc, nl):
    n = t.shape[0]
    nt = n // TM
    const = lambda *shape: pl.BlockSpec(shape, lambda i: (0,) * len(shape))
    lay = lambda *shape: pl.BlockSpec((None,) + shape, lambda i: (layer,) + (0,) * len(shape))
    return pl.pallas_call(
        _proj_kernel,
        grid=(nt,),
        in_specs=[pl.BlockSpec((TM, D_MODEL), lambda i: (i, 0)),
                  pl.BlockSpec((None, None, 6, D_MODEL), lambda i: (layer, i // nl, 0, 0)),
                  lay(1, D_MODEL),
                  lay(D_MODEL, IN_W),
                  pl.BlockSpec((TM, 2 * HEAD_DIM), lambda i: (i, 0)),
                  pl.BlockSpec((TM, 2 * HEAD_DIM), lambda i: (i, 0)),
                  lay(1, ATTN_W), lay(1, KV_W),
                  const(ATTN_W, ATTN_W), const(ATTN_W, ATTN_W),
                  lay(1, SGU_W), lay(CHUNK, SGU_GROUPS * CHUNK), lay(CHUNK, SGU_W),
                  const(FNET_W, 2 * FNET_W)],
        out_specs=[pl.BlockSpec((N_Q_HEADS, HEAD_DIM, TM), lambda i: (0, 0, i)),
                   pl.BlockSpec((N_KV_HEADS, TM, HEAD_DIM), lambda i: (0, i, 0)),
                   pl.BlockSpec((N_KV_HEADS, TM // LANES, V_ROWS, LANES), lambda i: (0, i, 0, 0)),
                   pl.BlockSpec((TM, SGU_W), lambda i: (i, 0)),
                   pl.BlockSpec((TM, 2 * FNET_W), lambda i: (i, 0))],
        out_shape=[jax.ShapeDtypeStruct((N_Q_HEADS, HEAD_DIM, n), BF16),
                   jax.ShapeDtypeStruct((N_KV_HEADS, n, HEAD_DIM), BF16),
                   jax.ShapeDtypeStruct((N_KV_HEADS, n // LANES, V_ROWS, LANES), BF16),
                   jax.ShapeDtypeStruct((n, SGU_W), BF16),
                   jax.ShapeDtypeStruct((n, 2 * FNET_W), BF16)],
        compiler_params=_params(("arbitrary",)),
        name="proj",
    )(t, mods, norm1_g, w_in_bf, cos_t, sin_t, qg, kg, msq, rot, sgug, wcat, sbias, dftc)


def _attn_kernel(qt_ref, k_ref, vt_ref, o_ref, *st_sc, n_lat_tiles, n_lat_chunks, ctx_len, lat_len):
    i = pl.program_id(1)
    bufs = (st_sc[:Q_PER_KV], st_sc[Q_PER_KV:])

    def scores(chunk, buf, g):
        start, nk = chunk
        buf[g][:nk, :] = _dot(k_ref[start:start + nk, :], qt_ref[g])

    def softmax_pv(chunk, buf, g, m_old, acc):
        start, nk = chunk
        vt = jnp.concatenate([vt_ref[start // LANES + c] for c in range(nk // LANES)], axis=1)
        st = buf[g][:nk, :]
        m_new = jnp.maximum(m_old, jnp.max(st, axis=0, keepdims=True))
        alpha = jnp.exp2(m_old - m_new)
        pt = jnp.exp2(st - m_new).astype(BF16)
        return m_new, alpha * acc + _dot(vt, pt)

    def run(chunks):
        carry = [(jnp.full((1, TM), -jnp.inf, F32), jnp.zeros((V_ROWS, TM), F32)) for _ in range(Q_PER_KV)]
        for g in range(Q_PER_KV):
            scores(chunks[0], bufs[0], g)
        for c, chunk in enumerate(chunks):
            cur, nxt = bufs[c % 2], bufs[(c + 1) % 2]
            for g in range(Q_PER_KV):
                if c + 1 < len(chunks):
                    scores(chunks[c + 1], nxt, g)
                carry[g] = softmax_pv(chunk, cur, g, *carry[g])
        o2 = jnp.concatenate([acc[:HEAD_DIM, :] / acc[HEAD_DIM:HEAD_DIM + 1, :] for _, acc in carry],
                             axis=0)
        o_ref[...] = o2.T.astype(BF16)

    ctx_chunk = (lat_len, ctx_len)

    @pl.when(i < n_lat_tiles)
    def _():
        run([(j * TK, TK) for j in range(n_lat_chunks)] + [ctx_chunk])

    @pl.when(i >= n_lat_tiles)
    def _():
        run([ctx_chunk])


def _attention(qt, k, vt, lat_len, ctx_len):
    n = k.shape[1]
    nt = n // TM
    kern = functools.partial(_attn_kernel, n_lat_tiles=lat_len // TM, n_lat_chunks=lat_len // TK,
                             ctx_len=ctx_len, lat_len=lat_len)
    return pl.pallas_call(
        kern,
        grid=(N_KV_HEADS, nt),
        in_specs=[pl.BlockSpec((Q_PER_KV, HEAD_DIM, TM), lambda h, i: (h, 0, i)),
                  pl.BlockSpec((None, n, HEAD_DIM), lambda h, i: (h, 0, 0)),
                  pl.BlockSpec((None, n // LANES, V_ROWS, LANES), lambda h, i: (h, 0, 0, 0))],
        out_specs=pl.BlockSpec((TM, Q_PER_KV * HEAD_DIM), lambda h, i: (i, h)),
        out_shape=jax.ShapeDtypeStruct((n, ATTN_W), BF16),
        scratch_shapes=[pltpu.VMEM((TK, TM), F32) for _ in range(2 * Q_PER_KV)],
        compiler_params=_params(("arbitrary", "arbitrary")),
        name="attention",
    )(qt, k, vt)


FFT_BATCH = 4


def _fft1_kernel(g_ref, cs_ref, twc_ref, tws_ref, o_ref):
    n1 = g_ref.shape[0]
    w = 2 * FNET_W
    for b in range(FFT_BATCH):
        xg = g_ref[:, b * w:(b + 1) * w]
        pq = _dot(cs_ref[...], xg)
        ar = pq[:n1, :FNET_W] + pq[n1:, FNET_W:]
        ai = pq[:n1, FNET_W:] - pq[n1:, :FNET_W]
        c = jnp.concatenate([twc_ref[b]] * (FNET_W // LANES), axis=1)
        s = jnp.concatenate([tws_ref[b]] * (FNET_W // LANES), axis=1)
        o_ref[:, b * w:b * w + FNET_W] = (ar * c + ai * s).astype(BF16)
        o_ref[:, b * w + FNET_W:(b + 1) * w] = (ai * c - ar * s).astype(BF16)


def _fft2_kernel(t_ref, cs_ref, o_ref):
    for b in range(FFT_BATCH):
        tb = t_ref[b * LANES:(b + 1) * LANES, :]
        st = jnp.concatenate([tb[:, :FNET_W], tb[:, FNET_W:]], axis=0)
        o_ref[:, b * FNET_W:(b + 1) * FNET_W] = _dot(cs_ref[...], st).astype(BF16)


def _fftc_kernel(g_ref, cs_ref, o_ref):
    gb = g_ref[...]
    st = jnp.concatenate([gb[:, :FNET_W], gb[:, FNET_W:]], axis=0)
    o_ref[...] = _dot(cs_ref[...], st).astype(BF16)


def _fourier_positions(g, lat_len, ctx_len, consts):
    n1 = lat_len // LANES
    w = 2 * FNET_W
    g_lat = g[:lat_len].reshape(n1, LANES * w)
    t = pl.pallas_call(
        _fft1_kernel,
        grid=(LANES // FFT_BATCH,),
        in_specs=[pl.BlockSpec((n1, FFT_BATCH * w), lambda j: (0, j)),
                  pl.BlockSpec((2 * n1, n1), lambda j: (0, 0)),
                  pl.BlockSpec((FFT_BATCH, n1, LANES), lambda j: (j, 0, 0)),
                  pl.BlockSpec((FFT_BATCH, n1, LANES), lambda j: (j, 0, 0))],
        out_specs=pl.BlockSpec((n1, FFT_BATCH * w), lambda j: (0, j)),
        out_shape=jax.ShapeDtypeStruct((n1, LANES * w), BF16),
        compiler_params=_params(("arbitrary",)),
        name="fft_stage1",
    )(g_lat, consts['cs1'], consts['twc'], consts['tws'])
    y_lat = pl.pallas_call(
        _fft2_kernel,
        grid=(n1 // FFT_BATCH,),
        in_specs=[pl.BlockSpec((FFT_BATCH * LANES, w), lambda j: (j, 0)),
                  pl.BlockSpec((LANES, 2 * LANES), lambda j: (0, 0))],
        out_specs=pl.BlockSpec((LANES, FFT_BATCH * FNET_W), lambda j: (0, j)),
        out_shape=jax.ShapeDtypeStruct((LANES, n1 * FNET_W), BF16),
        compiler_params=_params(("arbitrary",)),
        name="fft_stage2",
    )(t.reshape(n1 * LANES, w), consts['cs2'])
    y_ctx = pl.pallas_call(
        _fftc_kernel,
        grid=(1,),
        in_specs=[pl.BlockSpec((ctx_len, w), lambda j: (lat_len // ctx_len, 0)),
                  pl.BlockSpec((ctx_len, 2 * ctx_len), lambda j: (0, 0))],
        out_specs=pl.BlockSpec((ctx_len, FNET_W), lambda j: (0, 0)),
        out_shape=jax.ShapeDtypeStruct((ctx_len, FNET_W), BF16),
        compiler_params=_params(("arbitrary",)),
        name="fft_ctx",
    )(g, consts['csc'])
    return y_lat.reshape(lat_len, FNET_W), y_ctx


def _outproj_kernel(x_ref, mod_ref, o_ref, sgu_ref, yl_ref, yc_ref, wout_ref, g2_ref, rw_ref,
                    xo_ref, h2_ref, aff_ref, *, n_lat_tiles):
    i = pl.program_id(0)
    yf = jnp.where(i < n_lat_tiles, yl_ref[...], yc_ref[...])
    mix = _dot(o_ref[...], wout_ref[:ATTN_W, :])
    mix = mix + _dot(sgu_ref[...], wout_ref[ATTN_W:ATTN_W + SGU_W, :])
    mix = mix + _dot(yf, wout_ref[ATTN_W + SGU_W:, :])
    x = x_ref[...] + mod_ref[2:3, :] * mix
    xo_ref[...] = x
    h = x * lax.rsqrt(jnp.mean(x * x, axis=-1, keepdims=True) + EPS) * g2_ref[...]
    h = h * (1.0 + mod_ref[4:5, :]) + mod_ref[3:4, :]
    h2_ref[...] = h
    logits = _dot_x3(h, rw_ref[...])
    expert_lane = lax.broadcasted_iota(jnp.int32, logits.shape, 1) < N_EXPERTS
    logits = jnp.where(expert_lane, logits, -jnp.inf)
    z = jnp.exp(logits - jnp.max(logits, axis=-1, keepdims=True))
    aff_ref[...] = z / jnp.sum(z, axis=-1, keepdims=True)


def _outproj(layer, t, mods, o, sgu, y_lat, y_ctx, w_out_bf, norm2_g, rw, nl):
    n = t.shape[0]
    nt = n // TM
    lay = lambda *shape: pl.BlockSpec((None,) + shape, lambda i: (layer,) + (0,) * len(shape))
    return pl.pallas_call(
        functools.partial(_outproj_kernel, n_lat_tiles=nl),
        grid=(nt,),
        in_specs=[pl.BlockSpec((TM, D_MODEL), lambda i: (i, 0)),
                  pl.BlockSpec((None, None, 6, D_MODEL), lambda i: (layer, i // nl, 0, 0)),
                  pl.BlockSpec((TM, ATTN_W), lambda i: (i, 0)),
                  pl.BlockSpec((TM, SGU_W), lambda i: (i, 0)),
                  pl.BlockSpec((TM, FNET_W), lambda i: (jnp.minimum(i, nl - 1), 0)),
                  pl.BlockSpec((TM, FNET_W), lambda i: (0, 0)),
                  lay(D_MODEL, D_MODEL), lay(1, D_MODEL), lay(D_MODEL, LANES)],
        out_specs=[pl.BlockSpec((TM, D_MODEL), lambda i: (i, 0)),
                   pl.BlockSpec((TM, D_MODEL), lambda i: (i, 0)),
                   pl.BlockSpec((TM, LANES), lambda i: (i, 0))],
        out_shape=[jax.ShapeDtypeStruct((n, D_MODEL), F32),
                   jax.ShapeDtypeStruct((n, D_MODEL), F32),
                   jax.ShapeDtypeStruct((n, LANES), F32)],
        compiler_params=_params(("arbitrary",)),
        name="outproj_router",
    )(t, mods, o, sgu, y_lat, y_ctx, w_out_bf, norm2_g, rw)


def _route_kernel(a_ref, ut_ref, lt_ref, slt_ref, slot_ref, off_ref, idx_ref, *, rows, cap, capw):
    e_n = N_EXPERTS
    a3 = a_ref[...].reshape(e_n, rows, LANES)

    def count(mask):
        c = jnp.sum(jnp.where(mask, 1.0, 0.0), axis=1, keepdims=True)
        return jnp.sum(c, axis=2, keepdims=True)

    def as_float(bits):
        return lax.bitcast_convert_type(bits, F32)

    def search(it, thr):
        cand = thr | jnp.left_shift(jnp.int32(1), 30 - it)
        return jnp.where(count(a3 >= as_float(cand)) >= cap, cand, thr)

    thr = lax.fori_loop(0, 31, search, jnp.zeros((e_n, 1, 1), jnp.int32))
    above = as_float(thr + 1)
    gt = a3 >= above
    eq = jnp.logical_and(a3 >= as_float(thr), a3 < above)
    need = cap - count(gt)

    def ranks(mask3):
        m = jnp.where(mask3, 1.0, 0.0).reshape(e_n * rows, LANES)
        cin = _dot(m.astype(BF16), ut_ref[...])
        tot = cin[:, LANES - 1:LANES]
        totb = jnp.broadcast_to(tot, (e_n * rows, LANES)).astype(BF16)
        offs = [_dot(slt_ref[...], totb[e * rows:(e + 1) * rows]) for e in range(e_n)]
        off = jnp.concatenate(offs, axis=0)
        return m, cin, off

    m_eq, cin_eq, off_eq = ranks(eq)
    eq_rank = (off_eq + cin_eq - m_eq).reshape(e_n, rows, LANES)
    sel = gt | (eq & (eq_rank < need))
    m_sel, cin_sel, off_sel = ranks(sel)
    pos = off_sel + cin_sel - m_sel
    slot_ref[...] = jnp.where(m_sel > 0.5, pos, -1.0).astype(jnp.int32)
    off_ref[...] = off_sel.astype(jnp.int32)

    r_lane = lax.broadcasted_iota(jnp.int32, (1, capw), 1).astype(F32)
    row_iota = lax.broadcasted_iota(jnp.int32, (rows, capw), 0).astype(F32)
    for e in range(e_n):
        sl = slice(e * rows, (e + 1) * rows)
        off_e = off_sel[sl, 0:1]
        end_e = off_e + cin_sel[sl, LANES - 1:LANES]
        a_of_r = jnp.sum(jnp.where(end_e <= r_lane, 1.0, 0.0), axis=0, keepdims=True)
        onehot = row_iota == a_of_r
        off_at = jnp.sum(jnp.where(onehot, off_e, 0.0), axis=0, keepdims=True)
        cin_t = _dot_nt(lt_ref[...], m_sel[sl].astype(BF16))
        crow = _dot(cin_t.astype(BF16), jnp.where(onehot, 1.0, 0.0).astype(BF16))
        b_of_r = jnp.sum(jnp.where(crow <= r_lane - off_at, 1.0, 0.0), axis=0, keepdims=True)
        idx_ref[e:e + 1, :] = (a_of_r * LANES + b_of_r).astype(jnp.int32)


def _route(aff2, rows, cap, ut, lt, slt):
    capw = max(cap, LANES)
    full = lambda *shape: pl.BlockSpec(shape, lambda i: (0,) * len(shape))
    return pl.pallas_call(
        functools.partial(_route_kernel, rows=rows, cap=cap, capw=capw),
        grid=(1,),
        in_specs=[full(N_EXPERTS * rows, LANES), full(LANES, LANES), full(LANES, LANES), full(rows, rows)],
        out_specs=[full(N_EXPERTS * rows, LANES), full(N_EXPERTS * rows, LANES), full(N_EXPERTS, capw)],
        out_shape=[jax.ShapeDtypeStruct((N_EXPERTS * rows, LANES), jnp.int32),
                   jax.ShapeDtypeStruct((N_EXPERTS * rows, LANES), jnp.int32),
                   jax.ShapeDtypeStruct((N_EXPERTS, capw), jnp.int32)],
        compiler_params=_params(("arbitrary",)),
        name="route",
    )(aff2, ut, lt, slt)


def _gather_kernel(idx_ref, h_hbm, o_ref, buf, sem, *, rb):
    n_steps = pl.num_programs(0) * pl.num_programs(1)
    s = pl.program_id(0) * pl.num_programs(1) + pl.program_id(1)
    slot = s % 2

    def issue_block(step, to_slot):
        base = step * rb

        def issue(r, carry):
            tok = idx_ref[base + r]
            pltpu.make_async_copy(h_hbm.at[pl.ds(tok, 1), :], buf.at[to_slot, pl.ds(r, 1), :],
                                  sem.at[to_slot]).start()
            return carry

        lax.fori_loop(0, rb, issue, 0, unroll=GATHER_UNROLL)

    @pl.when(s == 0)
    def _():
        issue_block(0, 0)

    @pl.when(s + 1 < n_steps)
    def _():
        issue_block(s + 1, 1 - slot)

    pltpu.make_async_copy(h_hbm.at[pl.ds(0, rb), :], buf.at[slot], sem.at[slot]).wait()
    o_ref[...] = buf[slot].astype(BF16)


def _row_block(capt, limit=512):
    return max(r for r in range(BF16_ROWS, limit + 1, BF16_ROWS) if capt % r == 0)


def _gather(idx_flat, h2, capt):
    rb = _row_block(capt)
    return pl.pallas_call(
        functools.partial(_gather_kernel, rb=rb),
        grid_spec=pltpu.PrefetchScalarGridSpec(
            num_scalar_prefetch=1,
            grid=(N_EXPERTS, capt // rb),
            in_specs=[pl.BlockSpec(memory_space=pl.ANY)],
            out_specs=pl.BlockSpec((None, rb, D_MODEL), lambda e, r, idx: (e, r, 0)),
            scratch_shapes=[pltpu.VMEM((2, rb, D_MODEL), F32), pltpu.SemaphoreType.DMA((2,))]),
        out_shape=jax.ShapeDtypeStruct((N_EXPERTS, capt, D_MODEL), BF16),
        compiler_params=_params(("arbitrary", "arbitrary")),
        name="expert_gather",
    )(idx_flat, h2)


def _ffn_kernel(xs_ref, wg_ref, wu_ref, wd_ref, y_ref, acc_ref, *, rc):
    j = pl.program_id(1)
    wg = wg_ref[...].astype(BF16)
    wu = wu_ref[...].astype(BF16)
    wd = wd_ref[...].astype(BF16)
    capt = xs_ref.shape[0]

    @pl.when(j == 0)
    def _():
        acc_ref[...] = jnp.zeros(acc_ref.shape, F32)

    for r in range(capt // rc):
        sl = slice(r * rc, (r + 1) * rc)
        xs = xs_ref[sl, :]
        a = _dot(xs, wg)
        u = _dot(xs, wu)
        hmid = (a * _sigmoid(a) * u).astype(BF16)
        acc_ref[sl, :] += _dot(hmid, wd)

    @pl.when(j == pl.num_programs(1) - 1)
    def _():
        y_ref[...] = acc_ref[...].astype(BF16)


def _ffn(layer, xs, w_gate, w_up, w_down):
    capt = xs.shape[1]
    rc = _row_block(capt, FFN_ROWS_MAX)
    return pl.pallas_call(
        functools.partial(_ffn_kernel, rc=rc),
        grid=(N_EXPERTS, EXPERT_FF // FF_TILE),
        in_specs=[pl.BlockSpec((None, capt, D_MODEL), lambda e, j: (e, 0, 0)),
                  pl.BlockSpec((None, None, D_MODEL, FF_TILE), lambda e, j: (layer, e, 0, j)),
                  pl.BlockSpec((None, None, D_MODEL, FF_TILE), lambda e, j: (layer, e, 0, j)),
                  pl.BlockSpec((None, None, FF_TILE, D_MODEL), lambda e, j: (layer, e, j, 0))],
        out_specs=pl.BlockSpec((None, capt, D_MODEL), lambda e, j: (e, 0, 0)),
        out_shape=jax.ShapeDtypeStruct((N_EXPERTS, capt, D_MODEL), BF16),
        scratch_shapes=[pltpu.VMEM((capt, D_MODEL), F32)],
        compiler_params=_params(("arbitrary", "arbitrary")),
        name="expert_ffn",
    )(xs, w_gate, w_up, w_down)


def _combine_kernel(start_ref, extra_ref, x_ref, mod_ref, slot_ref, aff_ref, fg_ref, y_hbm, o_ref,
                    y_sc, acc_sc, sem, *, e0, ne, capt, final):
    i = pl.program_id(0)

    @pl.when(i == 0)
    def _():
        y_sc[:, capt:, :] = jnp.zeros((ne, y_sc.shape[1] - capt, D_MODEL), BF16)
        cp = pltpu.make_async_copy(y_hbm.at[pl.ds(e0, ne)], y_sc.at[:, pl.ds(0, capt), :], sem)
        cp.start()
        cp.wait()

    lane = lax.broadcasted_iota(jnp.int32, (TM, COMBINE_BLK), 1)

    def block(e, row0):
        rel = slot_ref[:, e0 + e:e0 + e + 1] - row0
        onehot = jnp.where(rel == lane, aff_ref[:, e0 + e:e0 + e + 1], 0.0).astype(BF16)
        return _dot(onehot, y_sc[e, pl.ds(row0, COMBINE_BLK), :])

    def start(e):
        return pl.multiple_of(start_ref[i * N_EXPERTS + e0 + e], COMBINE_ALIGN)

    acc = block(0, start(0))
    for e in range(1, ne):
        acc = acc + block(e, start(e))
    acc_sc[...] = acc
    for e in range(ne):
        for kb in range(1, COMBINE_MAX_BLKS):
            @pl.when(extra_ref[i * N_EXPERTS + e0 + e] >= kb)
            def _():
                acc_sc[...] += block(e, start(e) + kb * COMBINE_BLK)

    x = x_ref[...] + mod_ref[5:6, :] * acc_sc[...]
    if final:
        x = x * lax.rsqrt(jnp.mean(x * x, axis=-1, keepdims=True) + EPS) * fg_ref[...]
    o_ref[...] = x


def _combine(layer, starts, extras, t, mods, slot_t, aff_t, final_g, y, nl, e0, ne, nt, final):
    capt = y.shape[1]
    ypad = -(-capt // COMBINE_BLK) * COMBINE_BLK
    return pl.pallas_call(
        functools.partial(_combine_kernel, e0=e0, ne=ne, capt=capt, final=final),
        grid_spec=pltpu.PrefetchScalarGridSpec(
            num_scalar_prefetch=2,
            grid=(nt,),
            in_specs=[pl.BlockSpec((TM, D_MODEL), lambda i, s, x: (i, 0)),
                      pl.BlockSpec((None, None, 6, D_MODEL), lambda i, s, x: (layer, i // nl, 0, 0)),
                      pl.BlockSpec((TM, N_EXPERTS), lambda i, s, x: (i, 0)),
                      pl.BlockSpec((TM, LANES), lambda i, s, x: (i, 0)),
                      pl.BlockSpec((1, D_MODEL), lambda i, s, x: (0, 0)),
                      pl.BlockSpec(memory_space=pl.ANY)],
            out_specs=pl.BlockSpec((TM, D_MODEL), lambda i, s, x: (i, 0)),
            scratch_shapes=[pltpu.VMEM((ne, ypad, D_MODEL), BF16), pltpu.VMEM((TM, D_MODEL), F32),
                            pltpu.SemaphoreType.DMA(())]),
        out_shape=jax.ShapeDtypeStruct((nt * TM, D_MODEL), F32),
        compiler_params=_params(("arbitrary",)),
        name="combine",
    )(starts, extras, t, mods, slot_t, aff_t, final_g, y)


def _constants(lat_len, ctx_len):
    n1 = lat_len // LANES
    c64, s64 = _dft_cs(FNET_GROUP_W)
    groups = FNET_W // FNET_GROUP_W
    dftc = np.concatenate([_blockdiag(c64, groups), -_blockdiag(s64, groups)], axis=1) / math.sqrt(FNET_GROUP_W)
    c1, s1 = _dft_cs(n1)
    cs1 = np.concatenate([c1, s1], axis=0) / math.sqrt(n1)
    c2, s2 = _dft_cs(LANES)
    cs2 = np.concatenate([c2, s2], axis=1) / math.sqrt(LANES)
    cc, sc = _dft_cs(ctx_len)
    csc = np.concatenate([cc, sc], axis=1) / math.sqrt(ctx_len)
    n2 = jnp.arange(LANES, dtype=F32)[:, None]
    k1 = jnp.arange(n1, dtype=F32)[None, :]
    theta = (n2 * k1) * (2.0 * math.pi / lat_len)
    bcast = lambda a: jnp.broadcast_to(a[:, :, None], (LANES, n1, LANES))
    tri = np.triu(np.ones((LANES, LANES)))
    dft = lambda a: jnp.asarray(a, F32).astype(BF16)
    return dict(
        dftc=dft(dftc), cs1=dft(cs1), cs2=dft(cs2), csc=dft(csc),
        twc=bcast(jnp.cos(theta)), tws=bcast(jnp.sin(theta)),
        msq=jnp.asarray(_blockdiag(np.full((HEAD_DIM, HEAD_DIM), 1.0 / HEAD_DIM), N_Q_HEADS), BF16),
        rot=jnp.asarray(_rot_matrix(ATTN_W), BF16),
        ut=jnp.asarray(tri, BF16), lt=jnp.asarray(tri.T, BF16),
        slt_lat=jnp.asarray(np.tril(np.ones((n1, n1)), -1), BF16),
        slt_ctx=jnp.asarray(np.tril(np.ones((CTX_ROUTE_ROWS, CTX_ROUTE_ROWS)), -1), BF16),
    )


def kernel(x, c, ctx, c_ctx, w_ada, b_ada, norm1_g, w_in, q_norm_g, k_norm_g, sgu_norm_g, sgu_w, sgu_b,
           w_out, norm2_g, router_w, w_gate, w_up, w_down, final_g):
    assert x.shape[0] == 1 and ctx.shape[0] == 1
    lat_len, ctx_len = x.shape[1], ctx.shape[1]
    assert ctx_len == TM and lat_len % TK == 0 and lat_len % (LANES * FFT_BATCH) == 0
    n = lat_len + ctx_len
    nl = lat_len // TM
    n1 = lat_len // LANES
    cap_l = CAPACITY_FACTOR * lat_len // N_EXPERTS
    cap_c = CAPACITY_FACTOR * ctx_len // N_EXPERTS
    capt = cap_l + cap_c
    assert cap_c <= LANES and capt % BF16_ROWS == 0 and N_EXPERTS % COMBINE_EXPERTS == 0
    ypad = -(-capt // COMBINE_BLK) * COMBINE_BLK
    depth = w_ada.shape[0]
    consts = _constants(lat_len, ctx_len)
    cos_t, sin_t = _rope_tables(lat_len, ctx_len)

    cvec = jnp.zeros((16, D_MODEL), F32).at[0].set(c[0]).at[1].set(c_ctx)
    mods = _mods(cvec, w_ada, b_ada)[:, :2].reshape(depth, 2, 6, D_MODEL)

    w_in_bf = w_in.astype(BF16)
    w_out_bf = w_out.astype(BF16)
    qg = jnp.tile(q_norm_g, (1, N_Q_HEADS)).reshape(depth, 1, ATTN_W)
    kg = jnp.tile(k_norm_g, (1, N_KV_HEADS)).reshape(depth, 1, KV_W)
    wcat = jnp.transpose(sgu_w, (0, 2, 1, 3)).reshape(depth, CHUNK, SGU_GROUPS * CHUNK).astype(BF16)
    sbias = jnp.repeat(jnp.transpose(sgu_b, (0, 2, 1)), SGU_GROUP_W, axis=2)
    rw = jnp.pad(router_w, ((0, 0), (0, 0), (0, LANES - N_EXPERTS)))
    g1 = norm1_g.reshape(depth, 1, D_MODEL)
    g2 = norm2_g.reshape(depth, 1, D_MODEL)
    sgug = sgu_norm_g.reshape(depth, 1, SGU_W)
    fg = final_g.reshape(1, D_MODEL)

    t = jnp.concatenate([x[0], ctx[0]], axis=0)
    for layer in range(depth):
        final = layer == depth - 1
        qt, k, vt, sgu, g = _proj(layer, t, mods, g1, w_in_bf, cos_t, sin_t, qg, kg, consts['msq'], consts['rot'],
                                  sgug, wcat, sbias, consts['dftc'], nl)
        o = _attention(qt, k, vt, lat_len, ctx_len)
        y_lat, y_ctx = _fourier_positions(g, lat_len, ctx_len, consts)
        t, h2, aff_tok = _outproj(layer, t, mods, o, sgu, y_lat, y_ctx, w_out_bf, g2, rw, nl)
        aff = aff_tok[:, :N_EXPERTS].T

        aff_lat = aff[:, :lat_len].reshape(N_EXPERTS * n1, LANES)
        aff_ctx = jnp.pad(aff[:, lat_len:].reshape(N_EXPERTS, ctx_len // LANES, LANES),
                          ((0, 0), (0, CTX_ROUTE_ROWS - ctx_len // LANES), (0, 0))
                          ).reshape(N_EXPERTS * CTX_ROUTE_ROWS, LANES)
        slot_l, off_l, idx_l = _route(aff_lat, n1, cap_l, consts['ut'], consts['lt'], consts['slt_lat'])
        slot_c, _, idx_c = _route(aff_ctx, CTX_ROUTE_ROWS, cap_c, consts['ut'], consts['lt'], consts['slt_ctx'])

        idx = jnp.concatenate([idx_l[:, :cap_l], idx_c[:, :cap_c] + lat_len], axis=1)
        slot_c = slot_c.reshape(N_EXPERTS, CTX_ROUTE_ROWS * LANES)[:, :ctx_len]
        slot = jnp.concatenate([slot_l.reshape(N_EXPERTS, lat_len),
                                jnp.where(slot_c >= 0, slot_c + cap_l, -1)], axis=1)
        lo = off_l.reshape(N_EXPERTS, n1, LANES)[:, ::TM // LANES, 0]
        lo = jnp.concatenate([lo, jnp.full((N_EXPERTS, 1), cap_l, jnp.int32)], axis=1)
        hi = jnp.concatenate([lo[:, 1:], jnp.full((N_EXPERTS, 1), capt, jnp.int32)], axis=1)
        starts = jnp.minimum((lo // COMBINE_ALIGN) * COMBINE_ALIGN, ypad - COMBINE_BLK)
        extras = jnp.clip((hi - starts + COMBINE_BLK - 1) // COMBINE_BLK - 1, 0, COMBINE_MAX_BLKS - 1)

        xs = _gather(idx.reshape(-1), h2, capt)
        y = _ffn(layer, xs, w_gate, w_up, w_down)
        nt_out = nl if final else n // TM
        for e0 in range(0, N_EXPERTS, COMBINE_EXPERTS):
            last = e0 + COMBINE_EXPERTS == N_EXPERTS
            t = _combine(layer, starts.T.reshape(-1), extras.T.reshape(-1), t, mods, slot.T, aff_tok, fg, y, nl,
                         e0, COMBINE_EXPERTS, nt_out, final and last)
    return t.reshape(1, lat_len, D_MODEL)
```

```python
import functools
import math

import numpy as np
import jax
import jax.numpy as jnp
from jax import lax
from jax.experimental import pallas as pl
from jax.experimental.pallas import tpu as pltpu

D_MODEL = 1024
DEPTH = 4
GRID_W = 64
HEAD_DIM = 64
ATTN_W = 512
N_Q_HEADS = 8
N_KV_HEADS = 2
Q_PER_KV = 4
KV_W = 128
AXIS_DIM = 32
ROPE_BASE = 10000.0
SGU_W = 256
SGU_GROUP_W = 64
SGU_GROUPS = 4
CHUNK = 128
FNET_W = 256
FNET_GROUP_W = 64
IN_W = 1536
N_EXPERTS = 16
CAPACITY_FACTOR = 2
EXPERT_FF = 2048
EPS = 1e-6

LANES = 128
TM = 256
TK = 2048
FF_TILE = 512
FFN_ROWS_MAX = 1040
COMBINE_BLK = 128
COMBINE_ALIGN = 64
COMBINE_MAX_BLKS = -(-(COMBINE_ALIGN - 1 + TM) // COMBINE_BLK)
COMBINE_EXPERTS = 8
BF16_ROWS = 16
GATHER_UNROLL = 8
CTX_ROUTE_ROWS = 16
V_ROWS = HEAD_DIM + BF16_ROWS
LOG2E = math.log2(math.e)
VMEM_LIMIT = 56 * 1024 * 1024

F32 = jnp.float32
BF16 = jnp.bfloat16


def _dot(a, b):
    return jnp.dot(a, b, preferred_element_type=F32)


def _dot_nt(a, b):
    return lax.dot_general(a, b, (((1,), (1,)), ((), ())), preferred_element_type=F32)


def _split(a):
    hi = a.astype(BF16)
    lo = (a - hi.astype(F32)).astype(BF16)
    return hi, lo


def _dot_x2(a, b_bf16):
    hi, lo = _split(a)
    return _dot(hi, b_bf16) + _dot(lo, b_bf16)


def _dot_x3(a, b):
    ah, al = _split(a)
    bh, bl = _split(b)
    return _dot(ah, bh) + (_dot(al, bh) + _dot(ah, bl))


def _sigmoid(a):
    return 1.0 / (1.0 + jnp.exp(-a))


def _params(sem, vmem=VMEM_LIMIT):
    return pltpu.CompilerParams(dimension_semantics=sem, vmem_limit_bytes=vmem)


def _blockdiag(block, n):
    k = block.shape[0]
    out = np.zeros((k * n, block.shape[1] * n), np.float64)
    for i in range(n):
        out[i * k:(i + 1) * k, i * block.shape[1]:(i + 1) * block.shape[1]] = block
    return out


def _dft_cs(n):
    a = 2.0 * np.pi * np.outer(np.arange(n), np.arange(n)) / n
    return np.cos(a), np.sin(a)


def _rot_matrix(width):
    h = AXIS_DIM // 2
    r = np.zeros((width, width), np.float64)
    for j in range(width):
        if (j % AXIS_DIM) < h:
            r[j + h, j] = -1.0
        else:
            r[j - h, j] = 1.0
    return r


def _rope_tables(L, C):
    rows = L // GRID_W
    r, col = jnp.meshgrid(jnp.arange(rows, dtype=F32), jnp.arange(GRID_W, dtype=F32), indexing='ij')
    inv = 1.0 / (ROPE_BASE ** (jnp.arange(0, AXIS_DIM, 2, dtype=F32) / AXIS_DIM))
    ar = r.reshape(-1, 1) * inv
    ac = col.reshape(-1, 1) * inv
    cos = jnp.concatenate([jnp.cos(ar), jnp.cos(ar), jnp.cos(ac), jnp.cos(ac)], axis=-1)
    sin = jnp.concatenate([jnp.sin(ar), jnp.sin(ar), jnp.sin(ac), jnp.sin(ac)], axis=-1)
    cos = jnp.concatenate([cos, jnp.ones((C, HEAD_DIM), F32)], axis=0)
    sin = jnp.concatenate([sin, jnp.zeros((C, HEAD_DIM), F32)], axis=0)
    return jnp.tile(cos, (1, 2)), jnp.tile(sin, (1, 2))


def _mods_kernel(cv_ref, w_ref, b_ref, o_ref):
    cv = cv_ref[...]
    s = cv * _sigmoid(cv)
    o_ref[...] = _dot_x3(s, w_ref[...]) + b_ref[...]


def _mods(cvec, w_ada, b_ada):
    depth, d, six_d = w_ada.shape
    rows = cvec.shape[0]
    tn = 1024
    return pl.pallas_call(
        _mods_kernel,
        grid=(depth, six_d // tn),
        in_specs=[pl.BlockSpec((rows, d), lambda l, j: (0, 0)),
                  pl.BlockSpec((None, d, tn), lambda l, j: (l, 0, j)),
                  pl.BlockSpec((None, 1, tn), lambda l, j: (l, 0, j))],
        out_specs=pl.BlockSpec((None, rows, tn), lambda l, j: (l, 0, j)),
        out_shape=jax.ShapeDtypeStruct((depth, rows, six_d), F32),
        compiler_params=_params(("arbitrary", "arbitrary")),
        name="adaln_mods",
    )(cvec, w_ada, b_ada.reshape(depth, 1, six_d))


def _head_norm_rope(t, gain, cos, sin, msq, rot):
    ms = _dot_x2(t * t, msq)
    tn = t * lax.rsqrt(ms + EPS) * gain
    return tn * cos + _dot(tn.astype(BF16), rot) * sin


def _gelu_tanh(z):
    return 0.5 * z * (1.0 + jnp.tanh(math.sqrt(2.0 / math.pi) * (z + 0.044715 * (z * z * z))))


def _proj_kernel(x_ref, mod_ref, g1_ref, win_ref, cos_ref, sin_ref, qg_ref, kg_ref, msq_ref, rot_ref,
                 sgug_ref, wcat_ref, sbias_ref, dft_ref,
                 qt_ref, k_ref, vt_ref, sgu_ref, g_ref):
    x = x_ref[...]
    shift = mod_ref[0:1, :]
    scale = mod_ref[1:2, :]
    h = x * lax.rsqrt(jnp.mean(x * x, axis=-1, keepdims=True) + EPS) * g1_ref[...]
    h = h * (1.0 + scale) + shift
    p = _dot(h.astype(BF16), win_ref[...])

    cos2 = cos_ref[...]
    sin2 = sin_ref[...]
    cos8 = jnp.concatenate([cos2] * 4, axis=1)
    sin8 = jnp.concatenate([sin2] * 4, axis=1)
    qo = _head_norm_rope(p[:, :ATTN_W], qg_ref[...], cos8, sin8, msq_ref[...], rot_ref[...])
    qot = (qo * (HEAD_DIM ** -0.5 * LOG2E)).T
    for hh in range(N_Q_HEADS):
        qt_ref[hh] = qot[hh * HEAD_DIM:(hh + 1) * HEAD_DIM, :].astype(BF16)

    ko = _head_norm_rope(p[:, ATTN_W:ATTN_W + KV_W], kg_ref[...], cos2, sin2,
                         msq_ref[:KV_W, :KV_W], rot_ref[:KV_W, :KV_W])
    vt = p[:, ATTN_W + KV_W:ATTN_W + 2 * KV_W].T
    ones_rows = jnp.where(lax.broadcasted_iota(jnp.int32, (V_ROWS - HEAD_DIM, LANES), 0) == 0, 1.0, 0.0)
    for hh in range(N_KV_HEADS):
        k_ref[hh] = ko[:, hh * HEAD_DIM:(hh + 1) * HEAD_DIM].astype(BF16)
        for cc in range(TM // LANES):
            vt_ref[hh, cc, :HEAD_DIM, :] = vt[hh * HEAD_DIM:(hh + 1) * HEAD_DIM,
                                              cc * LANES:(cc + 1) * LANES].astype(BF16)
            vt_ref[hh, cc, HEAD_DIM:, :] = ones_rows.astype(BF16)

    o3 = ATTN_W + 2 * KV_W
    ge = _gelu_tanh(p[:, o3:o3 + 2 * SGU_W])
    u = ge[:, :SGU_W]
    vv = ge[:, SGU_W:]
    vms = _dot_x2(vv * vv, msq_ref[:SGU_W, :SGU_W])
    vn = vv * lax.rsqrt(vms + EPS) * sgug_ref[...]
    lane_group = lax.broadcasted_iota(jnp.int32, (CHUNK, SGU_W), 1) // SGU_GROUP_W
    for cc in range(TM // CHUNK):
        vc = vn[cc * CHUNK:(cc + 1) * CHUNK]
        vstack = jnp.concatenate([jnp.where(lane_group == gg, vc, 0.0) for gg in range(SGU_GROUPS)],
                                 axis=0).astype(BF16)
        s = _dot(wcat_ref[...], vstack) + sbias_ref[...]
        sgu_ref[cc * CHUNK:(cc + 1) * CHUNK, :] = (u[cc * CHUNK:(cc + 1) * CHUNK] * s).astype(BF16)

    o4 = o3 + 2 * SGU_W
    g_ref[...] = _dot(p[:, o4:].astype(BF16), dft_ref[...]).astype(BF16)


def _proj(layer, t, mods, norm1_g, w_in_bf, cos_t, sin_t, qg, kg, msq, rot, sgug, wcat, sbias, dftc, nl):
    n = t.shape[0]
    nt = n // TM
    const = lambda *shape: pl.BlockSpec(shape, lambda i: (0,) * len(shape))
    lay = lambda *shape: pl.BlockSpec((None,) + shape, lambda i: (layer,) + (0,) * len(shape))
    return pl.pallas_call(
        _proj_kernel,
        grid=(nt,),
        in_specs=[pl.BlockSpec((TM, D_MODEL), lambda i: (i, 0)),
                  pl.BlockSpec((None, None, 6, D_MODEL), lambda i: (layer, i // nl, 0, 0)),
                  lay(1, D_MODEL),
                  lay(D_MODEL, IN_W),
                  pl.BlockSpec((TM, 2 * HEAD_DIM), lambda i: (i, 0)),
                  pl.BlockSpec((TM, 2 * HEAD_DIM), lambda i: (i, 0)),
                  lay(1, ATTN_W), lay(1, KV_W),
                  const(ATTN_W, ATTN_W), const(ATTN_W, ATTN_W),
                  lay(1, SGU_W), lay(CHUNK, SGU_GROUPS * CHUNK), lay(CHUNK, SGU_W),
                  const(FNET_W, 2 * FNET_W)],
        out_specs=[pl.BlockSpec((N_Q_HEADS, HEAD_DIM, TM), lambda i: (0, 0, i)),
                   pl.BlockSpec((N_KV_HEADS, TM, HEAD_DIM), lambda i: (0, i, 0)),
                   pl.BlockSpec((N_KV_HEADS, TM // LANES, V_ROWS, LANES), lambda i: (0, i, 0, 0)),
                   pl.BlockSpec((TM, SGU_W), lambda i: (i, 0)),
                   pl.BlockSpec((TM, 2 * FNET_W), lambda i: (i, 0))],
        out_shape=[jax.ShapeDtypeStruct((N_Q_HEADS, HEAD_DIM, n), BF16),
                   jax.ShapeDtypeStruct((N_KV_HEADS, n, HEAD_DIM), BF16),
                   jax.ShapeDtypeStruct((N_KV_HEADS, n // LANES, V_ROWS, LANES), BF16),
                   jax.ShapeDtypeStruct((n, SGU_W), BF16),
                   jax.ShapeDtypeStruct((n, 2 * FNET_W), BF16)],
        compiler_params=_params(("arbitrary",)),
        name="proj",
    )(t, mods, norm1_g, w_in_bf, cos_t, sin_t, qg, kg, msq, rot, sgug, wcat, sbias, dftc)


def _attn_kernel(qt_ref, k_ref, vt_ref, o_ref, *st_sc, n_lat_tiles, n_lat_chunks, ctx_len, lat_len):
    i = pl.program_id(1)
    bufs = (st_sc[:Q_PER_KV], st_sc[Q_PER_KV:])

    def scores(chunk, buf, g):
        start, nk = chunk
        st = _dot(k_ref[start:start + nk, :], qt_ref[g])
        buf[g][:nk, :] = st
        return jnp.max(st, axis=0, keepdims=True)

    def softmax_pv(chunk, buf, g, st_max, m_old, acc):
        start, nk = chunk
        vt = jnp.concatenate([vt_ref[start // LANES + c] for c in range(nk // LANES)], axis=1)
        m_new = jnp.maximum(m_old, st_max)
        alpha = jnp.exp2(m_old - m_new)
        pt = jnp.exp2(buf[g][:nk, :] - m_new).astype(BF16)
        return m_new, alpha * acc + _dot(vt, pt)

    def run(chunks):
        carry = [(jnp.full((1, TM), -jnp.inf, F32), jnp.zeros((V_ROWS, TM), F32)) for _ in range(Q_PER_KV)]
        mx = [scores(chunks[0], bufs[0], g) for g in range(Q_PER_KV)]
        for c, chunk in enumerate(chunks):
            cur, nxt = bufs[c % 2], bufs[(c + 1) % 2]
            mx_next = list(mx)
            for g in range(Q_PER_KV):
                if c + 1 < len(chunks):
                    mx_next[g] = scores(chunks[c + 1], nxt, g)
                carry[g] = softmax_pv(chunk, cur, g, mx[g], *carry[g])
            mx = mx_next
        o2 = jnp.concatenate([acc[:HEAD_DIM, :] / acc[HEAD_DIM:HEAD_DIM + 1, :] for _, acc in carry],
                             axis=0)
        o_ref[...] = o2.T.astype(BF16)

    ctx_chunk = (lat_len, ctx_len)

    @pl.when(i < n_lat_tiles)
    def _():
        run([(j * TK, TK) for j in range(n_lat_chunks)] + [ctx_chunk])

    @pl.when(i >= n_lat_tiles)
    def _():
        run([ctx_chunk])


def _attention(qt, k, vt, lat_len, ctx_len):
    n = k.shape[1]
    nt = n // TM
    kern = functools.partial(_attn_kernel, n_lat_tiles=lat_len // TM, n_lat_chunks=lat_len // TK,
                             ctx_len=ctx_len, lat_len=lat_len)
    return pl.pallas_call(
        kern,
        grid=(N_KV_HEADS, nt),
        in_specs=[pl.BlockSpec((Q_PER_KV, HEAD_DIM, TM), lambda h, i: (h, 0, i)),
                  pl.BlockSpec((None, n, HEAD_DIM), lambda h, i: (h, 0, 0)),
                  pl.BlockSpec((None, n // LANES, V_ROWS, LANES), lambda h, i: (h, 0, 0, 0))],
        out_specs=pl.BlockSpec((TM, Q_PER_KV * HEAD_DIM), lambda h, i: (i, h)),
        out_shape=jax.ShapeDtypeStruct((n, ATTN_W), BF16),
        scratch_shapes=[pltpu.VMEM((TK, TM), F32) for _ in range(2 * Q_PER_KV)],
        compiler_params=_params(("arbitrary", "arbitrary")),
        name="attention",
    )(qt, k, vt)


FFT_BATCH = 4


def _fft1_kernel(g_ref, cs_ref, twc_ref, tws_ref, o_ref):
    n1 = g_ref.shape[0]
    w = 2 * FNET_W
    for b in range(FFT_BATCH):
        xg = g_ref[:, b * w:(b + 1) * w]
        pq = _dot(cs_ref[...], xg)
        ar = pq[:n1, :FNET_W] + pq[n1:, FNET_W:]
        ai = pq[:n1, FNET_W:] - pq[n1:, :FNET_W]
        c = jnp.concatenate([twc_ref[b]] * (FNET_W // LANES), axis=1)
        s = jnp.concatenate([tws_ref[b]] * (FNET_W // LANES), axis=1)
        o_ref[:, b * w:b * w + FNET_W] = (ar * c + ai * s).astype(BF16)
        o_ref[:, b * w + FNET_W:(b + 1) * w] = (ai * c - ar * s).astype(BF16)


def _fft2_kernel(t_ref, cs_ref, o_ref):
    for b in range(FFT_BATCH):
        tb = t_ref[b * LANES:(b + 1) * LANES, :]
        st = jnp.concatenate([tb[:, :FNET_W], tb[:, FNET_W:]], axis=0)
        o_ref[:, b * FNET_W:(b + 1) * FNET_W] = _dot(cs_ref[...], st).astype(BF16)


def _fftc_kernel(g_ref, cs_ref, o_ref):
    gb = g_ref[...]
    st = jnp.concatenate([gb[:, :FNET_W], gb[:, FNET_W:]], axis=0)
    o_ref[...] = _dot(cs_ref[...], st).astype(BF16)


def _fourier_positions(g, lat_len, ctx_len, consts):
    n1 = lat_len // LANES
    w = 2 * FNET_W
    g_lat = g[:lat_len].reshape(n1, LANES * w)
    t = pl.pallas_call(
        _fft1_kernel,
        grid=(LANES // FFT_BATCH,),
        in_specs=[pl.BlockSpec((n1, FFT_BATCH * w), lambda j: (0, j)),
                  pl.BlockSpec((2 * n1, n1), lambda j: (0, 0)),
                  pl.BlockSpec((FFT_BATCH, n1, LANES), lambda j: (j, 0, 0)),
                  pl.BlockSpec((FFT_BATCH, n1, LANES), lambda j: (j, 0, 0))],
        out_specs=pl.BlockSpec((n1, FFT_BATCH * w), lambda j: (0, j)),
        out_shape=jax.ShapeDtypeStruct((n1, LANES * w), BF16),
        compiler_params=_params(("arbitrary",)),
        name="fft_stage1",
    )(g_lat, consts['cs1'], consts['twc'], consts['tws'])
    y_lat = pl.pallas_call(
        _fft2_kernel,
        grid=(n1 // FFT_BATCH,),
        in_specs=[pl.BlockSpec((FFT_BATCH * LANES, w), lambda j: (j, 0)),
                  pl.BlockSpec((LANES, 2 * LANES), lambda j: (0, 0))],
        out_specs=pl.BlockSpec((LANES, FFT_BATCH * FNET_W), lambda j: (0, j)),
        out_shape=jax.ShapeDtypeStruct((LANES, n1 * FNET_W), BF16),
        compiler_params=_params(("arbitrary",)),
        name="fft_stage2",
    )(t.reshape(n1 * LANES, w), consts['cs2'])
    y_ctx = pl.pallas_call(
        _fftc_kernel,
        grid=(1,),
        in_specs=[pl.BlockSpec((ctx_len, w), lambda j: (lat_len // ctx_len, 0)),
                  pl.BlockSpec((ctx_len, 2 * ctx_len), lambda j: (0, 0))],
        out_specs=pl.BlockSpec((ctx_len, FNET_W), lambda j: (0, 0)),
        out_shape=jax.ShapeDtypeStruct((ctx_len, FNET_W), BF16),
        compiler_params=_params(("arbitrary",)),
        name="fft_ctx",
    )(g, consts['csc'])
    return y_lat.reshape(lat_len, FNET_W), y_ctx


def _outproj_kernel(x_ref, mod_ref, o_ref, sgu_ref, yl_ref, yc_ref, wout_ref, g2_ref, rw_ref,
                    xo_ref, h2_ref, aff_ref, *, n_lat_tiles):
    i = pl.program_id(0)
    yf = jnp.where(i < n_lat_tiles, yl_ref[...], yc_ref[...])
    mix = _dot(o_ref[...], wout_ref[:ATTN_W, :])
    mix = mix + _dot(sgu_ref[...], wout_ref[ATTN_W:ATTN_W + SGU_W, :])
    mix = mix + _dot(yf, wout_ref[ATTN_W + SGU_W:, :])
    x = x_ref[...] + mod_ref[2:3, :] * mix
    xo_ref[...] = x
    h = x * lax.rsqrt(jnp.mean(x * x, axis=-1, keepdims=True) + EPS) * g2_ref[...]
    h = h * (1.0 + mod_ref[4:5, :]) + mod_ref[3:4, :]
    h2_ref[...] = h
    logits = _dot_x3(h, rw_ref[...])
    expert_lane = lax.broadcasted_iota(jnp.int32, logits.shape, 1) < N_EXPERTS
    logits = jnp.where(expert_lane, logits, -jnp.inf)
    z = jnp.exp(logits - jnp.max(logits, axis=-1, keepdims=True))
    aff_ref[...] = z / jnp.sum(z, axis=-1, keepdims=True)


def _outproj(layer, t, mods, o, sgu, y_lat, y_ctx, w_out_bf, norm2_g, rw, nl):
    n = t.shape[0]
    nt = n // TM
    lay = lambda *shape: pl.BlockSpec((None,) + shape, lambda i: (layer,) + (0,) * len(shape))
    return pl.pallas_call(
        functools.partial(_outproj_kernel, n_lat_tiles=nl),
        grid=(nt,),
        in_specs=[pl.BlockSpec((TM, D_MODEL), lambda i: (i, 0)),
                  pl.BlockSpec((None, None, 6, D_MODEL), lambda i: (layer, i // nl, 0, 0)),
                  pl.BlockSpec((TM, ATTN_W), lambda i: (i, 0)),
                  pl.BlockSpec((TM, SGU_W), lambda i: (i, 0)),
                  pl.BlockSpec((TM, FNET_W), lambda i: (jnp.minimum(i, nl - 1), 0)),
                  pl.BlockSpec((TM, FNET_W), lambda i: (0, 0)),
                  lay(D_MODEL, D_MODEL), lay(1, D_MODEL), lay(D_MODEL, LANES)],
        out_specs=[pl.BlockSpec((TM, D_MODEL), lambda i: (i, 0)),
                   pl.BlockSpec((TM, D_MODEL), lambda i: (i, 0)),
                   pl.BlockSpec((TM, LANES), lambda i: (i, 0))],
        out_shape=[jax.ShapeDtypeStruct((n, D_MODEL), F32),
                   jax.ShapeDtypeStruct((n, D_MODEL), F32),
                   jax.ShapeDtypeStruct((n, LANES), F32)],
        compiler_params=_params(("arbitrary",)),
        name="outproj_router",
    )(t, mods, o, sgu, y_lat, y_ctx, w_out_bf, norm2_g, rw)


def _route_kernel(a_ref, ut_ref, lt_ref, slt_ref, slot_ref, off_ref, idx_ref, *, rows, cap, capw):
    e_n = N_EXPERTS
    a3 = a_ref[...].reshape(e_n, rows, LANES)

    def count(mask):
        c = jnp.sum(jnp.where(mask, 1.0, 0.0), axis=1, keepdims=True)
        return jnp.sum(c, axis=2, keepdims=True)

    def as_float(bits):
        return lax.bitcast_convert_type(bits, F32)

    def search(it, thr):
        cand = thr | jnp.left_shift(jnp.int32(1), 30 - it)
        return jnp.where(count(a3 >= as_float(cand)) >= cap, cand, thr)

    thr = lax.fori_loop(0, 31, search, jnp.zeros((e_n, 1, 1), jnp.int32))
    above = as_float(thr + 1)
    gt = a3 >= above
    eq = jnp.logical_and(a3 >= as_float(thr), a3 < above)
    need = cap - count(gt)

    def ranks(mask3):
        m = jnp.where(mask3, 1.0, 0.0).reshape(e_n * rows, LANES)
        cin = _dot(m.astype(BF16), ut_ref[...])
        tot = cin[:, LANES - 1:LANES]
        totb = jnp.broadcast_to(tot, (e_n * rows, LANES)).astype(BF16)
        offs = [_dot(slt_ref[...], totb[e * rows:(e + 1) * rows]) for e in range(e_n)]
        off = jnp.concatenate(offs, axis=0)
        return m, cin, off

    m_eq, cin_eq, off_eq = ranks(eq)
    eq_rank = (off_eq + cin_eq - m_eq).reshape(e_n, rows, LANES)
    sel = gt | (eq & (eq_rank < need))
    m_sel, cin_sel, off_sel = ranks(sel)
    pos = off_sel + cin_sel - m_sel
    slot_ref[...] = jnp.where(m_sel > 0.5, pos, -1.0).astype(jnp.int32)
    off_ref[...] = off_sel.astype(jnp.int32)

    r_lane = lax.broadcasted_iota(jnp.int32, (1, capw), 1).astype(F32)
    row_iota = lax.broadcasted_iota(jnp.int32, (rows, capw), 0).astype(F32)
    for e in range(e_n):
        sl = slice(e * rows, (e + 1) * rows)
        off_e = off_sel[sl, 0:1]
        end_e = off_e + cin_sel[sl, LANES - 1:LANES]
        a_of_r = jnp.sum(jnp.where(end_e <= r_lane, 1.0, 0.0), axis=0, keepdims=True)
        onehot = row_iota == a_of_r
        off_at = jnp.sum(jnp.where(onehot, off_e, 0.0), axis=0, keepdims=True)
        cin_t = _dot_nt(lt_ref[...], m_sel[sl].astype(BF16))
        crow = _dot(cin_t.astype(BF16), jnp.where(onehot, 1.0, 0.0).astype(BF16))
        b_of_r = jnp.sum(jnp.where(crow <= r_lane - off_at, 1.0, 0.0), axis=0, keepdims=True)
        idx_ref[e:e + 1, :] = (a_of_r * LANES + b_of_r).astype(jnp.int32)


def _route(aff2, rows, cap, ut, lt, slt):
    capw = max(cap, LANES)
    full = lambda *shape: pl.BlockSpec(shape, lambda i: (0,) * len(shape))
    return pl.pallas_call(
        functools.partial(_route_kernel, rows=rows, cap=cap, capw=capw),
        grid=(1,),
        in_specs=[full(N_EXPERTS * rows, LANES), full(LANES, LANES), full(LANES, LANES), full(rows, rows)],
        out_specs=[full(N_EXPERTS * rows, LANES), full(N_EXPERTS * rows, LANES), full(N_EXPERTS, capw)],
        out_shape=[jax.ShapeDtypeStruct((N_EXPERTS * rows, LANES), jnp.int32),
                   jax.ShapeDtypeStruct((N_EXPERTS * rows, LANES), jnp.int32),
                   jax.ShapeDtypeStruct((N_EXPERTS, capw), jnp.int32)],
        compiler_params=_params(("arbitrary",)),
        name="route",
    )(aff2, ut, lt, slt)


def _gather_kernel(idx_ref, h_hbm, o_ref, buf, sem, *, rb):
    n_steps = pl.num_programs(0) * pl.num_programs(1)
    s = pl.program_id(0) * pl.num_programs(1) + pl.program_id(1)
    slot = s % 2

    def issue_block(step, to_slot):
        base = step * rb

        def issue(r, carry):
            tok = idx_ref[base + r]
            pltpu.make_async_copy(h_hbm.at[pl.ds(tok, 1), :], buf.at[to_slot, pl.ds(r, 1), :],
                                  sem.at[to_slot]).start()
            return carry

        lax.fori_loop(0, rb, issue, 0, unroll=GATHER_UNROLL)

    @pl.when(s == 0)
    def _():
        issue_block(0, 0)

    @pl.when(s + 1 < n_steps)
    def _():
        issue_block(s + 1, 1 - slot)

    pltpu.make_async_copy(h_hbm.at[pl.ds(0, rb), :], buf.at[slot], sem.at[slot]).wait()
    o_ref[...] = buf[slot].astype(BF16)


def _row_block(capt, limit=512):
    return max(r for r in range(BF16_ROWS, limit + 1, BF16_ROWS) if capt % r == 0)


def _gather(idx_flat, h2, capt):
    rb = _row_block(capt)
    return pl.pallas_call(
        functools.partial(_gather_kernel, rb=rb),
        grid_spec=pltpu.PrefetchScalarGridSpec(
            num_scalar_prefetch=1,
            grid=(N_EXPERTS, capt // rb),
            in_specs=[pl.BlockSpec(memory_space=pl.ANY)],
            out_specs=pl.BlockSpec((None, rb, D_MODEL), lambda e, r, idx: (e, r, 0)),
            scratch_shapes=[pltpu.VMEM((2, rb, D_MODEL), F32), pltpu.SemaphoreType.DMA((2,))]),
        out_shape=jax.ShapeDtypeStruct((N_EXPERTS, capt, D_MODEL), BF16),
        compiler_params=_params(("arbitrary", "arbitrary")),
        name="expert_gather",
    )(idx_flat, h2)


def _ffn_kernel(xs_ref, wg_ref, wu_ref, wd_ref, y_ref, acc_ref, *, rc):
    j = pl.program_id(1)
    wg = wg_ref[...].astype(BF16)
    wu = wu_ref[...].astype(BF16)
    wd = wd_ref[...].astype(BF16)
    capt = xs_ref.shape[0]

    @pl.when(j == 0)
    def _():
        acc_ref[...] = jnp.zeros(acc_ref.shape, F32)

    for r in range(capt // rc):
        sl = slice(r * rc, (r + 1) * rc)
        xs = xs_ref[sl, :]
        a = _dot(xs, wg)
        u = _dot(xs, wu)
        hmid = (a * _sigmoid(a) * u).astype(BF16)
        acc_ref[sl, :] += _dot(hmid, wd)

    @pl.when(j == pl.num_programs(1) - 1)
    def _():
        y_ref[...] = acc_ref[...].astype(BF16)


def _ffn(layer, xs, w_gate, w_up, w_down):
    capt = xs.shape[1]
    rc = _row_block(capt, FFN_ROWS_MAX)
    return pl.pallas_call(
        functools.partial(_ffn_kernel, rc=rc),
        grid=(N_EXPERTS, EXPERT_FF // FF_TILE),
        in_specs=[pl.BlockSpec((None, capt, D_MODEL), lambda e, j: (e, 0, 0)),
                  pl.BlockSpec((None, None, D_MODEL, FF_TILE), lambda e, j: (layer, e, 0, j)),
                  pl.BlockSpec((None, None, D_MODEL, FF_TILE), lambda e, j: (layer, e, 0, j)),
                  pl.BlockSpec((None, None, FF_TILE, D_MODEL), lambda e, j: (layer, e, j, 0))],
        out_specs=pl.BlockSpec((None, capt, D_MODEL), lambda e, j: (e, 0, 0)),
        out_shape=jax.ShapeDtypeStruct((N_EXPERTS, capt, D_MODEL), BF16),
        scratch_shapes=[pltpu.VMEM((capt, D_MODEL), F32)],
        compiler_params=_params(("arbitrary", "arbitrary")),
        name="expert_ffn",
    )(xs, w_gate, w_up, w_down)


def _combine_kernel(start_ref, extra_ref, x_ref, mod_ref, slot_ref, aff_ref, fg_ref, y_hbm, o_ref,
                    y_sc, acc_sc, sem, *, e0, ne, capt, final):
    i = pl.program_id(0)

    @pl.when(i == 0)
    def _():
        y_sc[:, capt:, :] = jnp.zeros((ne, y_sc.shape[1] - capt, D_MODEL), BF16)
        cp = pltpu.make_async_copy(y_hbm.at[pl.ds(e0, ne)], y_sc.at[:, pl.ds(0, capt), :], sem)
        cp.start()
        cp.wait()

    lane = lax.broadcasted_iota(jnp.int32, (TM, COMBINE_BLK), 1)

    def block(e, row0):
        rel = slot_ref[:, e0 + e:e0 + e + 1] - row0
        onehot = jnp.where(rel == lane, aff_ref[:, e0 + e:e0 + e + 1], 0.0).astype(BF16)
        return _dot(onehot, y_sc[e, pl.ds(row0, COMBINE_BLK), :])

    def start(e):
        return pl.multiple_of(start_ref[i * N_EXPERTS + e0 + e], COMBINE_ALIGN)

    acc = block(0, start(0))
    for e in range(1, ne):
        acc = acc + block(e, start(e))

    def finish(moe):
        x = x_ref[...] + mod_ref[5:6, :] * moe
        if final:
            x = x * lax.rsqrt(jnp.mean(x * x, axis=-1, keepdims=True) + EPS) * fg_ref[...]
        o_ref[...] = x

    n_extra = extra_ref[i * N_EXPERTS + e0]
    for e in range(1, ne):
        n_extra = n_extra + extra_ref[i * N_EXPERTS + e0 + e]

    @pl.when(n_extra == 0)
    def _():
        finish(acc)

    @pl.when(n_extra > 0)
    def _():
        acc_sc[...] = acc
        for e in range(ne):
            for kb in range(1, COMBINE_MAX_BLKS):
                @pl.when(extra_ref[i * N_EXPERTS + e0 + e] >= kb)
                def _():
                    acc_sc[...] += block(e, start(e) + kb * COMBINE_BLK)
        finish(acc_sc[...])


def _combine(layer, starts, extras, t, mods, slot_t, aff_t, final_g, y, nl, e0, ne, nt, final):
    capt = y.shape[1]
    ypad = -(-capt // COMBINE_BLK) * COMBINE_BLK
    return pl.pallas_call(
        functools.partial(_combine_kernel, e0=e0, ne=ne, capt=capt, final=final),
        grid_spec=pltpu.PrefetchScalarGridSpec(
            num_scalar_prefetch=2,
            grid=(nt,),
            in_specs=[pl.BlockSpec((TM, D_MODEL), lambda i, s, x: (i, 0)),
                      pl.BlockSpec((None, None, 6, D_MODEL), lambda i, s, x: (layer, i // nl, 0, 0)),
                      pl.BlockSpec((TM, N_EXPERTS), lambda i, s, x: (i, 0)),
                      pl.BlockSpec((TM, LANES), lambda i, s, x: (i, 0)),
                      pl.BlockSpec((1, D_MODEL), lambda i, s, x: (0, 0)),
                      pl.BlockSpec(memory_space=pl.ANY)],
            out_specs=pl.BlockSpec((TM, D_MODEL), lambda i, s, x: (i, 0)),
            scratch_shapes=[pltpu.VMEM((ne, ypad, D_MODEL), BF16), pltpu.VMEM((TM, D_MODEL), F32),
                            pltpu.SemaphoreType.DMA(())]),
        out_shape=jax.ShapeDtypeStruct((nt * TM, D_MODEL), F32),
        compiler_params=_params(("arbitrary",)),
        name="combine",
    )(starts, extras, t, mods, slot_t, aff_t, final_g, y)


def _constants(lat_len, ctx_len):
    n1 = lat_len // LANES
    c64, s64 = _dft_cs(FNET_GROUP_W)
    groups = FNET_W // FNET_GROUP_W
    dftc = np.concatenate([_blockdiag(c64, groups), -_blockdiag(s64, groups)], axis=1) / math.sqrt(FNET_GROUP_W)
    c1, s1 = _dft_cs(n1)
    cs1 = np.concatenate([c1, s1], axis=0) / math.sqrt(n1)
    c2, s2 = _dft_cs(LANES)
    cs2 = np.concatenate([c2, s2], axis=1) / math.sqrt(LANES)
    cc, sc = _dft_cs(ctx_len)
    csc = np.concatenate([cc, sc], axis=1) / math.sqrt(ctx_len)
    n2 = jnp.arange(LANES, dtype=F32)[:, None]
    k1 = jnp.arange(n1, dtype=F32)[None, :]
    theta = (n2 * k1) * (2.0 * math.pi / lat_len)
    bcast = lambda a: jnp.broadcast_to(a[:, :, None], (LANES, n1, LANES))
    tri = np.triu(np.ones((LANES, LANES)))
    dft = lambda a: jnp.asarray(a, F32).astype(BF16)
    return dict(
        dftc=dft(dftc), cs1=dft(cs1), cs2=dft(cs2), csc=dft(csc),
        twc=bcast(jnp.cos(theta)), tws=bcast(jnp.sin(theta)),
        msq=jnp.asarray(_blockdiag(np.full((HEAD_DIM, HEAD_DIM), 1.0 / HEAD_DIM), N_Q_HEADS), BF16),
        rot=jnp.asarray(_rot_matrix(ATTN_W), BF16),
        ut=jnp.asarray(tri, BF16), lt=jnp.asarray(tri.T, BF16),
        slt_lat=jnp.asarray(np.tril(np.ones((n1, n1)), -1), BF16),
        slt_ctx=jnp.asarray(np.tril(np.ones((CTX_ROUTE_ROWS, CTX_ROUTE_ROWS)), -1), BF16),
    )


def kernel(x, c, ctx, c_ctx, w_ada, b_ada, norm1_g, w_in, q_norm_g, k_norm_g, sgu_norm_g, sgu_w, sgu_b,
           w_out, norm2_g, router_w, w_gate, w_up, w_down, final_g):
    assert x.shape[0] == 1 and ctx.shape[0] == 1
    lat_len, ctx_len = x.shape[1], ctx.shape[1]
    assert ctx_len == TM and lat_len % TK == 0 and lat_len % (LANES * FFT_BATCH) == 0
    n = lat_len + ctx_len
    nl = lat_len // TM
    n1 = lat_len // LANES
    cap_l = CAPACITY_FACTOR * lat_len // N_EXPERTS
    cap_c = CAPACITY_FACTOR * ctx_len // N_EXPERTS
    capt = cap_l + cap_c
    assert cap_c <= LANES and capt % BF16_ROWS == 0 and N_EXPERTS % COMBINE_EXPERTS == 0
    ypad = -(-capt // COMBINE_BLK) * COMBINE_BLK
    depth = w_ada.shape[0]
    consts = _constants(lat_len, ctx_len)
    cos_t, sin_t = _rope_tables(lat_len, ctx_len)

    cvec = jnp.zeros((16, D_MODEL), F32).at[0].set(c[0]).at[1].set(c_ctx)
    mods = _mods(cvec, w_ada, b_ada)[:, :2].reshape(depth, 2, 6, D_MODEL)

    w_in_bf = w_in.astype(BF16)
    w_out_bf = w_out.astype(BF16)
    qg = jnp.tile(q_norm_g, (1, N_Q_HEADS)).reshape(depth, 1, ATTN_W)
    kg = jnp.tile(k_norm_g, (1, N_KV_HEADS)).reshape(depth, 1, KV_W)
    wcat = jnp.transpose(sgu_w, (0, 2, 1, 3)).reshape(depth, CHUNK, SGU_GROUPS * CHUNK).astype(BF16)
    sbias = jnp.repeat(jnp.transpose(sgu_b, (0, 2, 1)), SGU_GROUP_W, axis=2)
    rw = jnp.pad(router_w, ((0, 0), (0, 0), (0, LANES - N_EXPERTS)))
    g1 = norm1_g.reshape(depth, 1, D_MODEL)
    g2 = norm2_g.reshape(depth, 1, D_MODEL)
    sgug = sgu_norm_g.reshape(depth, 1, SGU_W)
    fg = final_g.reshape(1, D_MODEL)

    t = jnp.concatenate([x[0], ctx[0]], axis=0)
    for layer in range(depth):
        final = layer == depth - 1
        qt, k, vt, sgu, g = _proj(layer, t, mods, g1, w_in_bf, cos_t, sin_t, qg, kg, consts['msq'], consts['rot'],
                                  sgug, wcat, sbias, consts['dftc'], nl)
        o = _attention(qt, k, vt, lat_len, ctx_len)
        y_lat, y_ctx = _fourier_positions(g, lat_len, ctx_len, consts)
        t, h2, aff_tok = _outproj(layer, t, mods, o, sgu, y_lat, y_ctx, w_out_bf, g2, rw, nl)
        aff = aff_tok[:, :N_EXPERTS].T

        aff_lat = aff[:, :lat_len].reshape(N_EXPERTS * n1, LANES)
        aff_ctx = jnp.pad(aff[:, lat_len:].reshape(N_EXPERTS, ctx_len // LANES, LANES),
                          ((0, 0), (0, CTX_ROUTE_ROWS - ctx_len // LANES), (0, 0))
                          ).reshape(N_EXPERTS * CTX_ROUTE_ROWS, LANES)
        slot_l, off_l, idx_l = _route(aff_lat, n1, cap_l, consts['ut'], consts['lt'], consts['slt_lat'])
        slot_c, _, idx_c = _route(aff_ctx, CTX_ROUTE_ROWS, cap_c, consts['ut'], consts['lt'], consts['slt_ctx'])

        idx = jnp.concatenate([idx_l[:, :cap_l], idx_c[:, :cap_c] + lat_len], axis=1)
        slot_c = slot_c.reshape(N_EXPERTS, CTX_ROUTE_ROWS * LANES)[:, :ctx_len]
        slot = jnp.concatenate([slot_l.reshape(N_EXPERTS, lat_len),
                                jnp.where(slot_c >= 0, slot_c + cap_l, -1)], axis=1)
        lo = off_l.reshape(N_EXPERTS, n1, LANES)[:, ::TM // LANES, 0]
        lo = jnp.concatenate([lo, jnp.full((N_EXPERTS, 1), cap_l, jnp.int32)], axis=1)
        hi = jnp.concatenate([lo[:, 1:], jnp.full((N_EXPERTS, 1), capt, jnp.int32)], axis=1)
        starts = jnp.minimum((lo // COMBINE_ALIGN) * COMBINE_ALIGN, ypad - COMBINE_BLK)
        extras = jnp.clip((hi - starts + COMBINE_BLK - 1) // COMBINE_BLK - 1, 0, COMBINE_MAX_BLKS - 1)

        xs = _gather(idx.reshape(-1), h2, capt)
        y = _ffn(layer, xs, w_gate, w_up, w_down)
        nt_out = nl if final else n // TM
        for e0 in range(0, N_EXPERTS, COMBINE_EXPERTS):
            last = e0 + COMBINE_EXPERTS == N_EXPERTS
            t = _combine(layer, starts.T.reshape(-1), extras.T.reshape(-1), t, mods, slot.T, aff_tok, fg, y, nl,
                         e0, COMBINE_EXPERTS, nt_out, final and last)
    return t.reshape(1, lat_len, D_MODEL)
```

```python
import functools
import math

import numpy as np
import jax
import jax.numpy as jnp
from jax import lax
from jax.experimental import pallas as pl
from jax.experimental.pallas import tpu as pltpu

D_MODEL = 1024
DEPTH = 4
GRID_W = 64
HEAD_DIM = 64
ATTN_W = 512
N_Q_HEADS = 8
N_KV_HEADS = 2
Q_PER_KV = 4
KV_W = 128
AXIS_DIM = 32
ROPE_BASE = 10000.0
SGU_W = 256
SGU_GROUP_W = 64
SGU_GROUPS = 4
CHUNK = 128
FNET_W = 256
FNET_GROUP_W = 64
IN_W = 1536
N_EXPERTS = 16
CAPACITY_FACTOR = 2
EXPERT_FF = 2048
EPS = 1e-6

LANES = 128
TM = 256
TK = 2048
FF_TILE = 512
FFN_ROWS_MAX = 1040
COMBINE_BLK = 128
COMBINE_ALIGN = 64
COMBINE_MAX_BLKS = -(-(COMBINE_ALIGN - 1 + TM) // COMBINE_BLK)
COMBINE_EXPERTS = 8
BF16_ROWS = 16
GATHER_UNROLL = 8
CTX_ROUTE_ROWS = 16
V_ROWS = HEAD_DIM + BF16_ROWS
LOG2E = math.log2(math.e)
VMEM_LIMIT = 56 * 1024 * 1024

F32 = jnp.float32
BF16 = jnp.bfloat16


def _dot(a, b):
    return jnp.dot(a, b, preferred_element_type=F32)


def _dot_nt(a, b):
    return lax.dot_general(a, b, (((1,), (1,)), ((), ())), preferred_element_type=F32)


def _split(a):
    hi = a.astype(BF16)
    lo = (a - hi.astype(F32)).astype(BF16)
    return hi, lo


def _dot_x2(a, b_bf16):
    hi, lo = _split(a)
    return _dot(hi, b_bf16) + _dot(lo, b_bf16)


def _dot_x3(a, b):
    ah, al = _split(a)
    bh, bl = _split(b)
    return _dot(ah, bh) + (_dot(al, bh) + _dot(ah, bl))


def _sigmoid(a):
    return 1.0 / (1.0 + jnp.exp(-a))


def _params(sem, vmem=VMEM_LIMIT):
    return pltpu.CompilerParams(dimension_semantics=sem, vmem_limit_bytes=vmem)


def _blockdiag(block, n):
    k = block.shape[0]
    out = np.zeros((k * n, block.shape[1] * n), np.float64)
    for i in range(n):
        out[i * k:(i + 1) * k, i * block.shape[1]:(i + 1) * block.shape[1]] = block
    return out


def _dft_cs(n):
    a = 2.0 * np.pi * np.outer(np.arange(n), np.arange(n)) / n
    return np.cos(a), np.sin(a)


def _rot_matrix(width):
    h = AXIS_DIM // 2
    r = np.zeros((width, width), np.float64)
    for j in range(width):
        if (j % AXIS_DIM) < h:
            r[j + h, j] = -1.0
        else:
            r[j - h, j] = 1.0
    return r


def _rope_tables(L, C):
    rows = L // GRID_W
    r, col = jnp.meshgrid(jnp.arange(rows, dtype=F32), jnp.arange(GRID_W, dtype=F32), indexing='ij')
    inv = 1.0 / (ROPE_BASE ** (jnp.arange(0, AXIS_DIM, 2, dtype=F32) / AXIS_DIM))
    ar = r.reshape(-1, 1) * inv
    ac = col.reshape(-1, 1) * inv
    cos = jnp.concatenate([jnp.cos(ar), jnp.cos(ar), jnp.cos(ac), jnp.cos(ac)], axis=-1)
    sin = jnp.concatenate([jnp.sin(ar), jnp.sin(ar), jnp.sin(ac), jnp.sin(ac)], axis=-1)
    cos = jnp.concatenate([cos, jnp.ones((C, HEAD_DIM), F32)], axis=0)
    sin = jnp.concatenate([sin, jnp.zeros((C, HEAD_DIM), F32)], axis=0)
    return jnp.tile(cos, (1, 2)), jnp.tile(sin, (1, 2))


def _mods_kernel(cv_ref, w_ref, b_ref, o_ref):
    cv = cv_ref[...]
    s = cv * _sigmoid(cv)
    o_ref[...] = _dot_x3(s, w_ref[...]) + b_ref[...]


def _mods(cvec, w_ada, b_ada):
    depth, d, six_d = w_ada.shape
    rows = cvec.shape[0]
    tn = 1024
    return pl.pallas_call(
        _mods_kernel,
        grid=(depth, six_d // tn),
        in_specs=[pl.BlockSpec((rows, d), lambda l, j: (0, 0)),
                  pl.BlockSpec((None, d, tn), lambda l, j: (l, 0, j)),
                  pl.BlockSpec((None, 1, tn), lambda l, j: (l, 0, j))],
        out_specs=pl.BlockSpec((None, rows, tn), lambda l, j: (l, 0, j)),
        out_shape=jax.ShapeDtypeStruct((depth, rows, six_d), F32),
        compiler_params=_params(("arbitrary", "arbitrary")),
        name="adaln_mods",
    )(cvec, w_ada, b_ada.reshape(depth, 1, six_d))


def _head_norm_rope(t, gain, cos, sin, msq, rot):
    ms = _dot_x2(t * t, msq)
    tn = t * lax.rsqrt(ms + EPS) * gain
    return tn * cos + _dot(tn.astype(BF16), rot) * sin


def _gelu_tanh(z):
    return 0.5 * z * (1.0 + jnp.tanh(math.sqrt(2.0 / math.pi) * (z + 0.044715 * (z * z * z))))


def _proj_kernel(x_ref, mod_ref, g1_ref, win_ref, cos_ref, sin_ref, qg_ref, kg_ref, msq_ref, rot_ref,
                 sgug_ref, wcat_ref, sbias_ref, dft_ref,
                 qt_ref, k_ref, vt_ref, sgu_ref, g_ref):
    x = x_ref[...]
    shift = mod_ref[0:1, :]
    scale = mod_ref[1:2, :]
    h = x * lax.rsqrt(jnp.mean(x * x, axis=-1, keepdims=True) + EPS) * g1_ref[...]
    h = h * (1.0 + scale) + shift
    p = _dot(h.astype(BF16), win_ref[...])

    cos2 = cos_ref[...]
    sin2 = sin_ref[...]
    cos8 = jnp.concatenate([cos2] * 4, axis=1)
    sin8 = jnp.concatenate([sin2] * 4, axis=1)
    qo = _head_norm_rope(p[:, :ATTN_W], qg_ref[...], cos8, sin8, msq_ref[...], rot_ref[...])
    qot = (qo * (HEAD_DIM ** -0.5 * LOG2E)).T
    for hh in range(N_Q_HEADS):
        qt_ref[hh] = qot[hh * HEAD_DIM:(hh + 1) * HEAD_DIM, :].astype(BF16)

    ko = _head_norm_rope(p[:, ATTN_W:ATTN_W + KV_W], kg_ref[...], cos2, sin2,
                         msq_ref[:KV_W, :KV_W], rot_ref[:KV_W, :KV_W])
    vt = p[:, ATTN_W + KV_W:ATTN_W + 2 * KV_W].T
    ones_rows = jnp.where(lax.broadcasted_iota(jnp.int32, (V_ROWS - HEAD_DIM, LANES), 0) == 0, 1.0, 0.0)
    for hh in range(N_KV_HEADS):
        k_ref[hh] = ko[:, hh * HEAD_DIM:(hh + 1) * HEAD_DIM].astype(BF16)
        for cc in range(TM // LANES):
            vt_ref[hh, cc, :HEAD_DIM, :] = vt[hh * HEAD_DIM:(hh + 1) * HEAD_DIM,
                                              cc * LANES:(cc + 1) * LANES].astype(BF16)
            vt_ref[hh, cc, HEAD_DIM:, :] = ones_rows.astype(BF16)

    o3 = ATTN_W + 2 * KV_W
    ge = _gelu_tanh(p[:, o3:o3 + 2 * SGU_W])
    u = ge[:, :SGU_W]
    vv = ge[:, SGU_W:]
    vms = _dot_x2(vv * vv, msq_ref[:SGU_W, :SGU_W])
    vn = vv * lax.rsqrt(vms + EPS) * sgug_ref[...]
    lane_group = lax.broadcasted_iota(jnp.int32, (CHUNK, SGU_W), 1) // SGU_GROUP_W
    for cc in range(TM // CHUNK):
        vc = vn[cc * CHUNK:(cc + 1) * CHUNK]
        vstack = jnp.concatenate([jnp.where(lane_group == gg, vc, 0.0) for gg in range(SGU_GROUPS)],
                                 axis=0).astype(BF16)
        s = _dot(wcat_ref[...], vstack) + sbias_ref[...]
        sgu_ref[cc * CHUNK:(cc + 1) * CHUNK, :] = (u[cc * CHUNK:(cc + 1) * CHUNK] * s).astype(BF16)

    o4 = o3 + 2 * SGU_W
    g_ref[...] = _dot(p[:, o4:].astype(BF16), dft_ref[...]).astype(BF16)


def _proj(layer, t, mods, norm1_g, w_in_bf, cos_t, sin_t, qg, kg, msq, rot, sgug, wcat, sbias, dftc, nl):
    n = t.shape[0]
    nt = n // TM
    const = lambda *shape: pl.BlockSpec(shape, lambda i: (0,) * len(shape))
    lay = lambda *shape: pl.BlockSpec((None,) + shape, lambda i: (layer,) + (0,) * len(shape))
    return pl.pallas_call(
        _proj_kernel,
        grid=(nt,),
        in_specs=[pl.BlockSpec((TM, D_MODEL), lambda i: (i, 0)),
                  pl.BlockSpec((None, None, 6, D_MODEL), lambda i: (layer, i // nl, 0, 0)),
                  lay(1, D_MODEL),
                  lay(D_MODEL, IN_W),
                  pl.BlockSpec((TM, 2 * HEAD_DIM), lambda i: (i, 0)),
                  pl.BlockSpec((TM, 2 * HEAD_DIM), lambda i: (i, 0)),
                  lay(1, ATTN_W), lay(1, KV_W),
                  const(ATTN_W, ATTN_W), const(ATTN_W, ATTN_W),
                  lay(1, SGU_W), lay(CHUNK, SGU_GROUPS * CHUNK), lay(CHUNK, SGU_W),
                  const(FNET_W, 2 * FNET_W)],
        out_specs=[pl.BlockSpec((N_Q_HEADS, HEAD_DIM, TM), lambda i: (0, 0, i)),
                   pl.BlockSpec((N_KV_HEADS, TM, HEAD_DIM), lambda i: (0, i, 0)),
                   pl.BlockSpec((N_KV_HEADS, TM // LANES, V_ROWS, LANES), lambda i: (0, i, 0, 0)),
                   pl.BlockSpec((TM, SGU_W), lambda i: (i, 0)),
                   pl.BlockSpec((TM, 2 * FNET_W), lambda i: (i, 0))],
        out_shape=[jax.ShapeDtypeStruct((N_Q_HEADS, HEAD_DIM, n), BF16),
                   jax.ShapeDtypeStruct((N_KV_HEADS, n, HEAD_DIM), BF16),
                   jax.ShapeDtypeStruct((N_KV_HEADS, n // LANES, V_ROWS, LANES), BF16),
                   jax.ShapeDtypeStruct((n, SGU_W), BF16),
                   jax.ShapeDtypeStruct((n, 2 * FNET_W), BF16)],
        compiler_params=_params(("arbitrary",)),
        name="proj",
    )(t, mods, norm1_g, w_in_bf, cos_t, sin_t, qg, kg, msq, rot, sgug, wcat, sbias, dftc)


def _attn_kernel(qt_ref, k_ref, vt_ref, o_ref, *st_sc, n_lat_tiles, n_lat_chunks, ctx_len, lat_len):
    i = pl.program_id(1)
    bufs = (st_sc[:Q_PER_KV], st_sc[Q_PER_KV:])

    def scores(chunk, buf, g):
        start, nk = chunk
        st = _dot(k_ref[start:start + nk, :], qt_ref[g])
        buf[g][:nk, :] = st
        return jnp.max(st, axis=0, keepdims=True)

    def softmax_pv(chunk, buf, g, st_max, m_old, acc):
        start, nk = chunk
        vt = jnp.concatenate([vt_ref[start // LANES + c] for c in range(nk // LANES)], axis=1)
        m_new = jnp.maximum(m_old, st_max)
        alpha = jnp.exp2(m_old - m_new)
        pt = jnp.exp2(buf[g][:nk, :] - m_new).astype(BF16)
        return m_new, alpha * acc + _dot(vt, pt)

    def run(chunks):
        carry = [(jnp.full((1, TM), -jnp.inf, F32), jnp.zeros((V_ROWS, TM), F32)) for _ in range(Q_PER_KV)]
        mx = [scores(chunks[0], bufs[0], g) for g in range(Q_PER_KV)]
        for c, chunk in enumerate(chunks):
            cur, nxt = bufs[c % 2], bufs[(c + 1) % 2]
            mx_next = list(mx)
            for g in range(Q_PER_KV):
                if c + 1 < len(chunks):
                    mx_next[g] = scores(chunks[c + 1], nxt, g)
                carry[g] = softmax_pv(chunk, cur, g, mx[g], *carry[g])
            mx = mx_next
        o2 = jnp.concatenate([acc[:HEAD_DIM, :] / acc[HEAD_DIM:HEAD_DIM + 1, :] for _, acc in carry],
                             axis=0)
        o_ref[...] = o2.T.astype(BF16)

    ctx_chunk = (lat_len, ctx_len)

    @pl.when(i < n_lat_tiles)
    def _():
        run([(j * TK, TK) for j in range(n_lat_chunks)] + [ctx_chunk])

    @pl.when(i >= n_lat_tiles)
    def _():
        run([ctx_chunk])


def _attention(qt, k, vt, lat_len, ctx_len):
    n = k.shape[1]
    nt = n // TM
    kern = functools.partial(_attn_kernel, n_lat_tiles=lat_len // TM, n_lat_chunks=lat_len // TK,
                             ctx_len=ctx_len, lat_len=lat_len)
    return pl.pallas_call(
        kern,
        grid=(N_KV_HEADS, nt),
        in_specs=[pl.BlockSpec((Q_PER_KV, HEAD_DIM, TM), lambda h, i: (h, 0, i)),
                  pl.BlockSpec((None, n, HEAD_DIM), lambda h, i: (h, 0, 0)),
                  pl.BlockSpec((None, n // LANES, V_ROWS, LANES), lambda h, i: (h, 0, 0, 0))],
        out_specs=pl.BlockSpec((TM, Q_PER_KV * HEAD_DIM), lambda h, i: (i, h)),
        out_shape=jax.ShapeDtypeStruct((n, ATTN_W), BF16),
        scratch_shapes=[pltpu.VMEM((TK, TM), F32) for _ in range(2 * Q_PER_KV)],
        compiler_params=_params(("arbitrary", "arbitrary")),
        name="attention",
    )(qt, k, vt)


FFT_BATCH = 4


def _fft1_kernel(g_ref, cs_ref, twc_ref, tws_ref, o_ref):
    n1 = g_ref.shape[0]
    w = 2 * FNET_W
    for b in range(FFT_BATCH):
        xg = g_ref[:, b * w:(b + 1) * w]
        pq = _dot(cs_ref[...], xg)
        ar = pq[:n1, :FNET_W] + pq[n1:, FNET_W:]
        ai = pq[:n1, FNET_W:] - pq[n1:, :FNET_W]
        c = jnp.concatenate([twc_ref[b]] * (FNET_W // LANES), axis=1)
        s = jnp.concatenate([tws_ref[b]] * (FNET_W // LANES), axis=1)
        o_ref[:, b * w:b * w + FNET_W] = (ar * c + ai * s).astype(BF16)
        o_ref[:, b * w + FNET_W:(b + 1) * w] = (ai * c - ar * s).astype(BF16)


def _fft2_kernel(t_ref, cs_ref, o_ref):
    for b in range(FFT_BATCH):
        tb = t_ref[b * LANES:(b + 1) * LANES, :]
        st = jnp.concatenate([tb[:, :FNET_W], tb[:, FNET_W:]], axis=0)
        o_ref[:, b * FNET_W:(b + 1) * FNET_W] = _dot(cs_ref[...], st).astype(BF16)


def _fftc_kernel(g_ref, cs_ref, o_ref):
    gb = g_ref[...]
    st = jnp.concatenate([gb[:, :FNET_W], gb[:, FNET_W:]], axis=0)
    o_ref[...] = _dot(cs_ref[...], st).astype(BF16)


def _fourier_positions(g, lat_len, ctx_len, consts):
    n1 = lat_len // LANES
    w = 2 * FNET_W
    g_lat = g[:lat_len].reshape(n1, LANES * w)
    t = pl.pallas_call(
        _fft1_kernel,
        grid=(LANES // FFT_BATCH,),
        in_specs=[pl.BlockSpec((n1, FFT_BATCH * w), lambda j: (0, j)),
                  pl.BlockSpec((2 * n1, n1), lambda j: (0, 0)),
                  pl.BlockSpec((FFT_BATCH, n1, LANES), lambda j: (j, 0, 0)),
                  pl.BlockSpec((FFT_BATCH, n1, LANES), lambda j: (j, 0, 0))],
        out_specs=pl.BlockSpec((n1, FFT_BATCH * w), lambda j: (0, j)),
        out_shape=jax.ShapeDtypeStruct((n1, LANES * w), BF16),
        compiler_params=_params(("arbitrary",)),
        name="fft_stage1",
    )(g_lat, consts['cs1'], consts['twc'], consts['tws'])
    y_lat = pl.pallas_call(
        _fft2_kernel,
        grid=(n1 // FFT_BATCH,),
        in_specs=[pl.BlockSpec((FFT_BATCH * LANES, w), lambda j: (j, 0)),
                  pl.BlockSpec((LANES, 2 * LANES), lambda j: (0, 0))],
        out_specs=pl.BlockSpec((LANES, FFT_BATCH * FNET_W), lambda j: (0, j)),
        out_shape=jax.ShapeDtypeStruct((LANES, n1 * FNET_W), BF16),
        compiler_params=_params(("arbitrary",)),
        name="fft_stage2",
    )(t.reshape(n1 * LANES, w), consts['cs2'])
    y_ctx = pl.pallas_call(
        _fftc_kernel,
        grid=(1,),
        in_specs=[pl.BlockSpec((ctx_len, w), lambda j: (lat_len // ctx_len, 0)),
                  pl.BlockSpec((ctx_len, 2 * ctx_len), lambda j: (0, 0))],
        out_specs=pl.BlockSpec((ctx_len, FNET_W), lambda j: (0, 0)),
        out_shape=jax.ShapeDtypeStruct((ctx_len, FNET_W), BF16),
        compiler_params=_params(("arbitrary",)),
        name="fft_ctx",
    )(g, consts['csc'])
    return y_lat.reshape(lat_len, FNET_W), y_ctx


def _outproj_kernel(x_ref, mod_ref, o_ref, sgu_ref, yl_ref, yc_ref, wout_ref, g2_ref, rw_ref,
                    xo_ref, h2_ref, aff_ref, *, n_lat_tiles):
    i = pl.program_id(0)
    yf = jnp.where(i < n_lat_tiles, yl_ref[...], yc_ref[...])
    mix = _dot(o_ref[...], wout_ref[:ATTN_W, :])
    mix = mix + _dot(sgu_ref[...], wout_ref[ATTN_W:ATTN_W + SGU_W, :])
    mix = mix + _dot(yf, wout_ref[ATTN_W + SGU_W:, :])
    x = x_ref[...] + mod_ref[2:3, :] * mix
    xo_ref[...] = x
    h = x * lax.rsqrt(jnp.mean(x * x, axis=-1, keepdims=True) + EPS) * g2_ref[...]
    h = h * (1.0 + mod_ref[4:5, :]) + mod_ref[3:4, :]
    h2_ref[...] = h
    logits = _dot_x3(h, rw_ref[...])
    expert_lane = lax.broadcasted_iota(jnp.int32, logits.shape, 1) < N_EXPERTS
    logits = jnp.where(expert_lane, logits, -jnp.inf)
    z = jnp.exp(logits - jnp.max(logits, axis=-1, keepdims=True))
    aff_ref[...] = z / jnp.sum(z, axis=-1, keepdims=True)


def _outproj(layer, t, mods, o, sgu, y_lat, y_ctx, w_out_bf, norm2_g, rw, nl):
    n = t.shape[0]
    nt = n // TM
    lay = lambda *shape: pl.BlockSpec((None,) + shape, lambda i: (layer,) + (0,) * len(shape))
    return pl.pallas_call(
        functools.partial(_outproj_kernel, n_lat_tiles=nl),
        grid=(nt,),
        in_specs=[pl.BlockSpec((TM, D_MODEL), lambda i: (i, 0)),
                  pl.BlockSpec((None, None, 6, D_MODEL), lambda i: (layer, i // nl, 0, 0)),
                  pl.BlockSpec((TM, ATTN_W), lambda i: (i, 0)),
                  pl.BlockSpec((TM, SGU_W), lambda i: (i, 0)),
                  pl.BlockSpec((TM, FNET_W), lambda i: (jnp.minimum(i, nl - 1), 0)),
                  pl.BlockSpec((TM, FNET_W), lambda i: (0, 0)),
                  lay(D_MODEL, D_MODEL), lay(1, D_MODEL), lay(D_MODEL, LANES)],
        out_specs=[pl.BlockSpec((TM, D_MODEL), lambda i: (i, 0)),
                   pl.BlockSpec((TM, D_MODEL), lambda i: (i, 0)),
                   pl.BlockSpec((TM, LANES), lambda i: (i, 0))],
        out_shape=[jax.ShapeDtypeStruct((n, D_MODEL), F32),
                   jax.ShapeDtypeStruct((n, D_MODEL), F32),
                   jax.ShapeDtypeStruct((n, LANES), F32)],
        compiler_params=_params(("arbitrary",)),
        name="outproj_router",
    )(t, mods, o, sgu, y_lat, y_ctx, w_out_bf, norm2_g, rw)


def _route_kernel(a_ref, ut_ref, lt_ref, slt_ref, slot_ref, off_ref, idx_ref, *, rows, cap, capw):
    e_n = N_EXPERTS
    a3 = a_ref[...].reshape(e_n, rows, LANES)

    def count(mask):
        c = jnp.sum(jnp.where(mask, 1.0, 0.0), axis=1, keepdims=True)
        return jnp.sum(c, axis=2, keepdims=True)

    def as_float(bits):
        return lax.bitcast_convert_type(bits, F32)

    def search(it, thr):
        cand = thr | jnp.left_shift(jnp.int32(1), 30 - it)
        return jnp.where(count(a3 >= as_float(cand)) >= cap, cand, thr)

    thr = lax.fori_loop(0, 31, search, jnp.zeros((e_n, 1, 1), jnp.int32))
    above = as_float(thr + 1)
    gt = a3 >= above
    eq = jnp.logical_and(a3 >= as_float(thr), a3 < above)
    need = cap - count(gt)

    def ranks(mask3):
        m = jnp.where(mask3, 1.0, 0.0).reshape(e_n * rows, LANES)
        cin = _dot(m.astype(BF16), ut_ref[...])
        tot = cin[:, LANES - 1:LANES]
        totb = jnp.broadcast_to(tot, (e_n * rows, LANES)).astype(BF16)
        offs = [_dot(slt_ref[...], totb[e * rows:(e + 1) * rows]) for e in range(e_n)]
        off = jnp.concatenate(offs, axis=0)
        return m, cin, off

    m_eq, cin_eq, off_eq = ranks(eq)
    eq_rank = (off_eq + cin_eq - m_eq).reshape(e_n, rows, LANES)
    sel = gt | (eq & (eq_rank < need))
    m_sel, cin_sel, off_sel = ranks(sel)
    pos = off_sel + cin_sel - m_sel
    slot_ref[...] = jnp.where(m_sel > 0.5, pos, -1.0).astype(jnp.int32)
    off_ref[...] = off_sel.astype(jnp.int32)

    r_lane = lax.broadcasted_iota(jnp.int32, (1, capw), 1).astype(F32)
    row_iota = lax.broadcasted_iota(jnp.int32, (rows, capw), 0).astype(F32)
    for e in range(e_n):
        sl = slice(e * rows, (e + 1) * rows)
        off_e = off_sel[sl, 0:1]
        end_e = off_e + cin_sel[sl, LANES - 1:LANES]
        a_of_r = jnp.sum(jnp.where(end_e <= r_lane, 1.0, 0.0), axis=0, keepdims=True)
        onehot = row_iota == a_of_r
        off_at = jnp.sum(jnp.where(onehot, off_e, 0.0), axis=0, keepdims=True)
        cin_t = _dot_nt(lt_ref[...], m_sel[sl].astype(BF16))
        crow = _dot(cin_t.astype(BF16), jnp.where(onehot, 1.0, 0.0).astype(BF16))
        b_of_r = jnp.sum(jnp.where(crow <= r_lane - off_at, 1.0, 0.0), axis=0, keepdims=True)
        idx_ref[e:e + 1, :] = (a_of_r * LANES + b_of_r).astype(jnp.int32)


def _route(aff2, rows, cap, ut, lt, slt):
    capw = max(cap, LANES)
    full = lambda *shape: pl.BlockSpec(shape, lambda i: (0,) * len(shape))
    return pl.pallas_call(
        functools.partial(_route_kernel, rows=rows, cap=cap, capw=capw),
        grid=(1,),
        in_specs=[full(N_EXPERTS * rows, LANES), full(LANES, LANES), full(LANES, LANES), full(rows, rows)],
        out_specs=[full(N_EXPERTS * rows, LANES), full(N_EXPERTS * rows, LANES), full(N_EXPERTS, capw)],
        out_shape=[jax.ShapeDtypeStruct((N_EXPERTS * rows, LANES), jnp.int32),
                   jax.ShapeDtypeStruct((N_EXPERTS * rows, LANES), jnp.int32),
                   jax.ShapeDtypeStruct((N_EXPERTS, capw), jnp.int32)],
        compiler_params=_params(("arbitrary",)),
        name="route",
    )(aff2, ut, lt, slt)


def _gather_kernel(idx_ref, h_hbm, o_ref, buf, sem, *, rb):
    n_steps = pl.num_programs(0) * pl.num_programs(1)
    s = pl.program_id(0) * pl.num_programs(1) + pl.program_id(1)
    slot = s % 2

    def issue_block(step, to_slot):
        base = step * rb

        def issue(r2, carry):
            for prio in range(2):
                r = 2 * r2 + prio
                tok = idx_ref[base + r]
                pltpu.make_async_copy(h_hbm.at[pl.ds(tok, 1), :], buf.at[to_slot, pl.ds(r, 1), :],
                                      sem.at[to_slot]).start(priority=prio)
            return carry

        lax.fori_loop(0, rb // 2, issue, 0, unroll=GATHER_UNROLL // 2)

    @pl.when(s == 0)
    def _():
        issue_block(0, 0)

    @pl.when(s + 1 < n_steps)
    def _():
        issue_block(s + 1, 1 - slot)

    pltpu.make_async_copy(h_hbm.at[pl.ds(0, rb), :], buf.at[slot], sem.at[slot]).wait()
    o_ref[...] = buf[slot].astype(BF16)


def _row_block(capt, limit=512):
    return max(r for r in range(BF16_ROWS, limit + 1, BF16_ROWS) if capt % r == 0)


def _gather(idx_flat, h2, capt):
    rb = _row_block(capt)
    return pl.pallas_call(
        functools.partial(_gather_kernel, rb=rb),
        grid_spec=pltpu.PrefetchScalarGridSpec(
            num_scalar_prefetch=1,
            grid=(N_EXPERTS, capt // rb),
            in_specs=[pl.BlockSpec(memory_space=pl.ANY)],
            out_specs=pl.BlockSpec((None, rb, D_MODEL), lambda e, r, idx: (e, r, 0)),
            scratch_shapes=[pltpu.VMEM((2, rb, D_MODEL), F32), pltpu.SemaphoreType.DMA((2,))]),
        out_shape=jax.ShapeDtypeStruct((N_EXPERTS, capt, D_MODEL), BF16),
        compiler_params=_params(("arbitrary", "arbitrary")),
        name="expert_gather",
    )(idx_flat, h2)


def _ffn_kernel(xs_ref, wg_ref, wu_ref, wd_ref, y_ref, acc_ref, *, rc):
    j = pl.program_id(1)
    wg = wg_ref[...].astype(BF16)
    wu = wu_ref[...].astype(BF16)
    wd = wd_ref[...].astype(BF16)
    capt = xs_ref.shape[0]

    @pl.when(j == 0)
    def _():
        acc_ref[...] = jnp.zeros(acc_ref.shape, F32)

    for r in range(capt // rc):
        sl = slice(r * rc, (r + 1) * rc)
        xs = xs_ref[sl, :]
        a = _dot(xs, wg)
        u = _dot(xs, wu)
        hmid = (a * _sigmoid(a) * u).astype(BF16)
        acc_ref[sl, :] += _dot(hmid, wd)

    @pl.when(j == pl.num_programs(1) - 1)
    def _():
        y_ref[...] = acc_ref[...].astype(BF16)


def _ffn(layer, xs, w_gate, w_up, w_down):
    capt = xs.shape[1]
    rc = _row_block(capt, FFN_ROWS_MAX)
    return pl.pallas_call(
        functools.partial(_ffn_kernel, rc=rc),
        grid=(N_EXPERTS, EXPERT_FF // FF_TILE),
        in_specs=[pl.BlockSpec((None, capt, D_MODEL), lambda e, j: (e, 0, 0)),
                  pl.BlockSpec((None, None, D_MODEL, FF_TILE), lambda e, j: (layer, e, 0, j)),
                  pl.BlockSpec((None, None, D_MODEL, FF_TILE), lambda e, j: (layer, e, 0, j)),
                  pl.BlockSpec((None, None, FF_TILE, D_MODEL), lambda e, j: (layer, e, j, 0))],
        out_specs=pl.BlockSpec((None, capt, D_MODEL), lambda e, j: (e, 0, 0)),
        out_shape=jax.ShapeDtypeStruct((N_EXPERTS, capt, D_MODEL), BF16),
        scratch_shapes=[pltpu.VMEM((capt, D_MODEL), F32)],
        compiler_params=_params(("arbitrary", "arbitrary")),
        name="expert_ffn",
    )(xs, w_gate, w_up, w_down)


def _combine_kernel(start_ref, extra_ref, x_ref, mod_ref, slot_ref, aff_ref, fg_ref, y_hbm, o_ref,
                    y_sc, acc_sc, sem, *, e0, ne, capt, final):
    i = pl.program_id(0)

    @pl.when(i == 0)
    def _():
        y_sc[:, capt:, :] = jnp.zeros((ne, y_sc.shape[1] - capt, D_MODEL), BF16)
        cp = pltpu.make_async_copy(y_hbm.at[pl.ds(e0, ne)], y_sc.at[:, pl.ds(0, capt), :], sem)
        cp.start()
        cp.wait()

    lane = lax.broadcasted_iota(jnp.int32, (TM, COMBINE_BLK), 1)

    def block(e, row0):
        rel = slot_ref[:, e0 + e:e0 + e + 1] - row0
        onehot = jnp.where(rel == lane, aff_ref[:, e0 + e:e0 + e + 1], 0.0).astype(BF16)
        return _dot(onehot, y_sc[e, pl.ds(row0, COMBINE_BLK), :])

    def start(e):
        return pl.multiple_of(start_ref[i * N_EXPERTS + e0 + e], COMBINE_ALIGN)

    acc = block(0, start(0))
    for e in range(1, ne):
        acc = acc + block(e, start(e))

    def finish(moe):
        x = x_ref[...] + mod_ref[5:6, :] * moe
        if final:
            x = x * lax.rsqrt(jnp.mean(x * x, axis=-1, keepdims=True) + EPS) * fg_ref[...]
        o_ref[...] = x

    n_extra = extra_ref[i * N_EXPERTS + e0]
    for e in range(1, ne):
        n_extra = n_extra + extra_ref[i * N_EXPERTS + e0 + e]

    @pl.when(n_extra == 0)
    def _():
        finish(acc)

    @pl.when(n_extra > 0)
    def _():
        acc_sc[...] = acc
        for e in range(ne):
            for kb in range(1, COMBINE_MAX_BLKS):
                @pl.when(extra_ref[i * N_EXPERTS + e0 + e] >= kb)
                def _():
                    acc_sc[...] += block(e, start(e) + kb * COMBINE_BLK)
        finish(acc_sc[...])


def _combine(layer, starts, extras, t, mods, slot_t, aff_t, final_g, y, nl, e0, ne, nt, final):
    capt = y.shape[1]
    ypad = -(-capt // COMBINE_BLK) * COMBINE_BLK
    return pl.pallas_call(
        functools.partial(_combine_kernel, e0=e0, ne=ne, capt=capt, final=final),
        grid_spec=pltpu.PrefetchScalarGridSpec(
            num_scalar_prefetch=2,
            grid=(nt,),
            in_specs=[pl.BlockSpec((TM, D_MODEL), lambda i, s, x: (i, 0)),
                      pl.BlockSpec((None, None, 6, D_MODEL), lambda i, s, x: (layer, i // nl, 0, 0)),
                      pl.BlockSpec((TM, N_EXPERTS), lambda i, s, x: (i, 0)),
                      pl.BlockSpec((TM, LANES), lambda i, s, x: (i, 0)),
                      pl.BlockSpec((1, D_MODEL), lambda i, s, x: (0, 0)),
                      pl.BlockSpec(memory_space=pl.ANY)],
            out_specs=pl.BlockSpec((TM, D_MODEL), lambda i, s, x: (i, 0)),
            scratch_shapes=[pltpu.VMEM((ne, ypad, D_MODEL), BF16), pltpu.VMEM((TM, D_MODEL), F32),
                            pltpu.SemaphoreType.DMA(())]),
        out_shape=jax.ShapeDtypeStruct((nt * TM, D_MODEL), F32),
        compiler_params=_params(("arbitrary",)),
        name="combine",
    )(starts, extras, t, mods, slot_t, aff_t, final_g, y)


def _constants(lat_len, ctx_len):
    n1 = lat_len // LANES
    c64, s64 = _dft_cs(FNET_GROUP_W)
    groups = FNET_W // FNET_GROUP_W
    dftc = np.concatenate([_blockdiag(c64, groups), -_blockdiag(s64, groups)], axis=1) / math.sqrt(FNET_GROUP_W)
    c1, s1 = _dft_cs(n1)
    cs1 = np.concatenate([c1, s1], axis=0) / math.sqrt(n1)
    c2, s2 = _dft_cs(LANES)
    cs2 = np.concatenate([c2, s2], axis=1) / math.sqrt(LANES)
    cc, sc = _dft_cs(ctx_len)
    csc = np.concatenate([cc, sc], axis=1) / math.sqrt(ctx_len)
    n2 = jnp.arange(LANES, dtype=F32)[:, None]
    k1 = jnp.arange(n1, dtype=F32)[None, :]
    theta = (n2 * k1) * (2.0 * math.pi / lat_len)
    bcast = lambda a: jnp.broadcast_to(a[:, :, None], (LANES, n1, LANES))
    tri = np.triu(np.ones((LANES, LANES)))
    dft = lambda a: jnp.asarray(a, F32).astype(BF16)
    return dict(
        dftc=dft(dftc), cs1=dft(cs1), cs2=dft(cs2), csc=dft(csc),
        twc=bcast(jnp.cos(theta)), tws=bcast(jnp.sin(theta)),
        msq=jnp.asarray(_blockdiag(np.full((HEAD_DIM, HEAD_DIM), 1.0 / HEAD_DIM), N_Q_HEADS), BF16),
        rot=jnp.asarray(_rot_matrix(ATTN_W), BF16),
        ut=jnp.asarray(tri, BF16), lt=jnp.asarray(tri.T, BF16),
        slt_lat=jnp.asarray(np.tril(np.ones((n1, n1)), -1), BF16),
        slt_ctx=jnp.asarray(np.tril(np.ones((CTX_ROUTE_ROWS, CTX_ROUTE_ROWS)), -1), BF16),
    )


def kernel(x, c, ctx, c_ctx, w_ada, b_ada, norm1_g, w_in, q_norm_g, k_norm_g, sgu_norm_g, sgu_w, sgu_b,
           w_out, norm2_g, router_w, w_gate, w_up, w_down, final_g):
    assert x.shape[0] == 1 and ctx.shape[0] == 1
    lat_len, ctx_len = x.shape[1], ctx.shape[1]
    assert ctx_len == TM and lat_len % TK == 0 and lat_len % (LANES * FFT_BATCH) == 0
    n = lat_len + ctx_len
    nl = lat_len // TM
    n1 = lat_len // LANES
    cap_l = CAPACITY_FACTOR * lat_len // N_EXPERTS
    cap_c = CAPACITY_FACTOR * ctx_len // N_EXPERTS
    capt = cap_l + cap_c
    assert cap_c <= LANES and capt % BF16_ROWS == 0 and N_EXPERTS % COMBINE_EXPERTS == 0
    ypad = -(-capt // COMBINE_BLK) * COMBINE_BLK
    depth = w_ada.shape[0]
    consts = _constants(lat_len, ctx_len)
    cos_t, sin_t = _rope_tables(lat_len, ctx_len)

    cvec = jnp.zeros((16, D_MODEL), F32).at[0].set(c[0]).at[1].set(c_ctx)
    mods = _mods(cvec, w_ada, b_ada)[:, :2].reshape(depth, 2, 6, D_MODEL)

    w_in_bf = w_in.astype(BF16)
    w_out_bf = w_out.astype(BF16)
    qg = jnp.tile(q_norm_g, (1, N_Q_HEADS)).reshape(depth, 1, ATTN_W)
    kg = jnp.tile(k_norm_g, (1, N_KV_HEADS)).reshape(depth, 1, KV_W)
    wcat = jnp.transpose(sgu_w, (0, 2, 1, 3)).reshape(depth, CHUNK, SGU_GROUPS * CHUNK).astype(BF16)
    sbias = jnp.repeat(jnp.transpose(sgu_b, (0, 2, 1)), SGU_GROUP_W, axis=2)
    rw = jnp.pad(router_w, ((0, 0), (0, 0), (0, LANES - N_EXPERTS)))
    g1 = norm1_g.reshape(depth, 1, D_MODEL)
    g2 = norm2_g.reshape(depth, 1, D_MODEL)
    sgug = sgu_norm_g.reshape(depth, 1, SGU_W)
    fg = final_g.reshape(1, D_MODEL)

    t = jnp.concatenate([x[0], ctx[0]], axis=0)
    for layer in range(depth):
        final = layer == depth - 1
        qt, k, vt, sgu, g = _proj(layer, t, mods, g1, w_in_bf, cos_t, sin_t, qg, kg, consts['msq'], consts['rot'],
                                  sgug, wcat, sbias, consts['dftc'], nl)
        o = _attention(qt, k, vt, lat_len, ctx_len)
        y_lat, y_ctx = _fourier_positions(g, lat_len, ctx_len, consts)
        t, h2, aff_tok = _outproj(layer, t, mods, o, sgu, y_lat, y_ctx, w_out_bf, g2, rw, nl)
        aff = aff_tok[:, :N_EXPERTS].T

        aff_lat = aff[:, :lat_len].reshape(N_EXPERTS * n1, LANES)
        aff_ctx = jnp.pad(aff[:, lat_len:].reshape(N_EXPERTS, ctx_len // LANES, LANES),
                          ((0, 0), (0, CTX_ROUTE_ROWS - ctx_len // LANES), (0, 0))
                          ).reshape(N_EXPERTS * CTX_ROUTE_ROWS, LANES)
        slot_l, off_l, idx_l = _route(aff_lat, n1, cap_l, consts['ut'], consts['lt'], consts['slt_lat'])
        slot_c, _, idx_c = _route(aff_ctx, CTX_ROUTE_ROWS, cap_c, consts['ut'], consts['lt'], consts['slt_ctx'])

        idx = jnp.concatenate([idx_l[:, :cap_l], idx_c[:, :cap_c] + lat_len], axis=1)
        slot_c = slot_c.reshape(N_EXPERTS, CTX_ROUTE_ROWS * LANES)[:, :ctx_len]
        slot = jnp.concatenate([slot_l.reshape(N_EXPERTS, lat_len),
                                jnp.where(slot_c >= 0, slot_c + cap_l, -1)], axis=1)
        lo = off_l.reshape(N_EXPERTS, n1, LANES)[:, ::TM // LANES, 0]
        lo = jnp.concatenate([lo, jnp.full((N_EXPERTS, 1), cap_l, jnp.int32)], axis=1)
        hi = jnp.concatenate([lo[:, 1:], jnp.full((N_EXPERTS, 1), capt, jnp.int32)], axis=1)
        starts = jnp.minimum((lo // COMBINE_ALIGN) * COMBINE_ALIGN, ypad - COMBINE_BLK)
        extras = jnp.clip((hi - starts + COMBINE_BLK - 1) // COMBINE_BLK - 1, 0, COMBINE_MAX_BLKS - 1)

        xs = _gather(idx.reshape(-1), h2, capt)
        y = _ffn(layer, xs, w_gate, w_up, w_down)
        nt_out = nl if final else n // TM
        for e0 in range(0, N_EXPERTS, COMBINE_EXPERTS):
            last = e0 + COMBINE_EXPERTS == N_EXPERTS
            t = _combine(layer, starts.T.reshape(-1), extras.T.reshape(-1), t, mods, slot.T, aff_tok, fg, y, nl,
                         e0, COMBINE_EXPERTS, nt_out, final and last)
    return t.reshape(1, lat_len, D_MODEL)
```
